```python
import math
import jax, jax.numpy as jnp
from jax import lax
import numpy as np

D_MODEL = 1024
BATCH = 32
SEQ = 2048
DEPTH = 2

HEAD_DIM = D_MODEL // 16
A_HEADS = 4
IDX_HEADS = 4
IDX_DIM = HEAD_DIM
TOPK_MAX = 256
B_HEADS = 4
DIFF_DH = HEAD_DIM // 2
C_HEADS = 8
C_CONFIGS = ((128, 1), (512, 4), (2048, 16))
C_BLOCK = 128
Q_BLOCK = 128
MIX_WIDTH = (A_HEADS + B_HEADS + C_HEADS) * HEAD_DIM
FF_DIM = -(-8 * D_MODEL // (3 * 256)) * 256
DEEPNORM_ALPHA = (2 * DEPTH) ** 0.25
DEEPNORM_BETA = (8 * DEPTH) ** -0.25
LN_EPS = 1e-5
IN_SIZES = (A_HEADS * HEAD_DIM, HEAD_DIM, HEAD_DIM,
            IDX_HEADS * IDX_DIM, IDX_DIM, IDX_HEADS,
            B_HEADS * 2 * DIFF_DH, B_HEADS * 2 * DIFF_DH, B_HEADS * HEAD_DIM,
            C_HEADS * HEAD_DIM, C_HEADS * HEAD_DIM, C_HEADS * HEAD_DIM)
IN_COLS = sum(IN_SIZES)

kernel_name = 'hybrid_dsa_diff_dilated_deepnorm'


def layer_norm(x, g, b):
    xf = x.astype(jnp.float32)
    mu = jnp.mean(xf, axis=-1, keepdims=True)
    var = jnp.mean(jnp.square(xf - mu), axis=-1, keepdims=True)
    y = (xf - mu) * lax.rsqrt(var + LN_EPS) * g.astype(jnp.float32) + b.astype(jnp.float32)
    return y.astype(x.dtype)


def alibi_slopes(n):
    return jnp.asarray([2.0 ** (-8.0 * (i + 1) / n) for i in range(n)], dtype=jnp.float32)


def to_blocks(a):
    b, s = a.shape[:2]
    a = a.reshape((b, s // Q_BLOCK, Q_BLOCK) + a.shape[2:])
    return jnp.moveaxis(a, 1, 0)


def from_blocks(a):
    nb, b, q = a.shape[:3]
    return jnp.moveaxis(a, 0, 1).reshape((b, nb * q) + a.shape[3:])


def split_columns(z):
    parts, start = [], 0
    for size in IN_SIZES:
        parts.append(z[..., start:start + size])
        start += size
    return parts


def dsa_attention(q, k, v, qi, ki, wi):
    s = q.shape[1]
    topk = min(TOPK_MAX, s // 4)
    slopes = alibi_slopes(A_HEADS)
    key_pos = jnp.arange(s)
    wi = wi.astype(jnp.float32) * IDX_HEADS ** -0.5

    def block(args):
        blk, q_b, qi_b, wi_b = args
        t = blk * Q_BLOCK + jnp.arange(Q_BLOCK)
        rel = jax.nn.relu(jnp.einsum('bqhd,bsd->bqhs', qi_b, ki).astype(jnp.float32) * IDX_DIM ** -0.5)
        score = jnp.einsum('bqhs,bqh->bqs', rel, wi_b)
        score = jnp.where(key_pos[None, None, :] <= t[None, :, None], score, -jnp.inf)
        _, sel = lax.top_k(score, topk)
        k_sel = jax.vmap(lambda kk, ii: kk[ii])(k, sel)
        v_sel = jax.vmap(lambda vv, ii: vv[ii])(v, sel)
        dist = (t[None, :, None] - sel).astype(jnp.float32)
        logits = jnp.einsum('bqhd,bqkd->bqhk', q_b, k_sel).astype(jnp.float32) * HEAD_DIM ** -0.5
        logits = logits - slopes[None, None, :, None] * dist[:, :, None, :]
        logits = jnp.where((dist >= 0)[:, :, None, :], logits, -jnp.inf)
        p = jax.nn.softmax(logits, axis=-1)
        return jnp.einsum('bqhk,bqkd->bqhd', p, v_sel.astype(jnp.float32)).astype(v.dtype)

    nb = s // Q_BLOCK
    out = lax.map(block, (jnp.arange(nb), to_blocks(q), to_blocks(qi), to_blocks(wi)))
    return from_blocks(out)


def diff_attention(q, k, v, lam, subln_g, lambda_init):
    s = q.shape[1]
    slopes = alibi_slopes(B_HEADS)
    lam = lam.astype(jnp.float32)
    lam_full = jnp.exp(jnp.sum(lam[0] * lam[1])) - jnp.exp(jnp.sum(lam[2] * lam[3])) + lambda_init
    key_pos = jnp.arange(s)
    vf = v.astype(jnp.float32)

    def block(args):
        blk, q_b = args
        t = blk * Q_BLOCK + jnp.arange(Q_BLOCK)
        dist = (t[:, None] - key_pos[None, :]).astype(jnp.float32)
        logits = jnp.einsum('bqhmd,bshmd->bhmqs', q_b, k).astype(jnp.float32) * DIFF_DH ** -0.5
        logits = logits - slopes[None, :, None, None, None] * dist
        logits = jnp.where(dist >= 0, logits, -jnp.inf)
        p = jax.nn.softmax(logits, axis=-1)
        w = p[:, :, 0] - lam_full * p[:, :, 1]
        return jnp.einsum('bhqs,bshd->bqhd', w, vf)

    nb = s // Q_BLOCK
    o = from_blocks(lax.map(block, (jnp.arange(nb), to_blocks(q))))
    o = o * lax.rsqrt(jnp.mean(o * o, axis=-1, keepdims=True) + LN_EPS) * subln_g.astype(jnp.float32)
    return (o * (1.0 - lambda_init)).astype(v.dtype)


def dilated_branch(q, k, v, dilation, steps, slopes):
    b, s, h, d = q.shape
    ls = s // dilation
    nb = -(-ls // C_BLOCK)
    lp = nb * C_BLOCK

    def sub(a):
        a = a.reshape(b, ls, dilation, h, d).transpose(0, 2, 1, 3, 4)
        a = jnp.pad(a, ((0, 0), (0, 0), (0, lp - ls), (0, 0), (0, 0)))
        return a.reshape(b, dilation, nb, C_BLOCK, h, d)

    def band(a):
        prev = jnp.pad(a, ((0, 0), (0, 0), (1, 0), (0, 0), (0, 0), (0, 0)))[:, :, :-1]
        return jnp.concatenate([prev, a], axis=3)

    qs = sub(q)
    kb, vb = band(sub(k)), band(sub(v))
    qi = jnp.arange(C_BLOCK) + C_BLOCK
    kj = jnp.arange(2 * C_BLOCK)
    delta = qi[:, None] - kj[None, :]
    key_idx = jnp.arange(nb)[:, None] * C_BLOCK + kj[None, :] - C_BLOCK
    valid = (delta >= 0)[None] & (delta <= steps)[None] & (key_idx >= 0)[:, None, :]
    dist = (delta * dilation).astype(jnp.float32)
    logits = jnp.einsum('bonqhd,bonkhd->bonhqk', qs, kb).astype(jnp.float32) * HEAD_DIM ** -0.5
    logits = logits - slopes[:, None, None] * dist
    logits = jnp.where(valid[:, None], logits, -jnp.inf)
    lse = jax.nn.logsumexp(logits, axis=-1)
    p = jnp.exp(logits - lse[..., None])
    o = jnp.einsum('bonhqk,bonkhd->bonqhd', p, vb.astype(jnp.float32))
    o = o.reshape(b, dilation, lp, h, d)[:, :, :ls].transpose(0, 2, 1, 3, 4).reshape(b, s, h, d)
    lse = lse.transpose(0, 1, 2, 4, 3).reshape(b, dilation, lp, h)[:, :, :ls]
    lse = lse.transpose(0, 2, 1, 3).reshape(b, s, h)
    return o, lse


def dilated_attention(q, k, v):
    slopes = alibi_slopes(C_HEADS)
    outs, lses = [], []
    for window, dil in C_CONFIGS:
        o, l = dilated_branch(q, k, v, dil, window // dil, slopes)
        outs.append(o)
        lses.append(l)
    wts = jax.nn.softmax(jnp.stack(lses), axis=0)
    o = jnp.sum(wts[..., None] * jnp.stack(outs), axis=0)
    return o.astype(v.dtype)


def hybrid_mixer(h, w_in, w_o, lam, subln_g, lambda_init):
    b, s, _ = h.shape
    p = split_columns(h @ w_in)
    qa = p[0].reshape(b, s, A_HEADS, HEAD_DIM)
    qi = p[3].reshape(b, s, IDX_HEADS, IDX_DIM)
    oa = dsa_attention(qa, p[1], p[2], qi, p[4], p[5]).reshape(b, s, -1)
    qb = p[6].reshape(b, s, B_HEADS, 2, DIFF_DH)
    kb = p[7].reshape(b, s, B_HEADS, 2, DIFF_DH)
    vb = p[8].reshape(b, s, B_HEADS, HEAD_DIM)
    ob = diff_attention(qb, kb, vb, lam, subln_g, lambda_init).reshape(b, s, -1)
    qc = p[9].reshape(b, s, C_HEADS, HEAD_DIM)
    kc = p[10].reshape(b, s, C_HEADS, HEAD_DIM)
    vc = p[11].reshape(b, s, C_HEADS, HEAD_DIM)
    oc = dilated_attention(qc, kc, vc).reshape(b, s, -1)
    return jnp.concatenate([oa, ob, oc], axis=-1) @ w_o


def setup_inputs(seed: int = 0) -> dict:
    key = jax.random.key(seed)
    ks = jax.random.split(key, 12)
    nrm = jax.random.normal
    x = nrm(ks[0], (BATCH, SEQ, D_MODEL), jnp.float32)
    w_in = nrm(ks[1], (DEPTH, D_MODEL, IN_COLS), jnp.float32) * D_MODEL ** -0.5
    w_o = nrm(ks[2], (DEPTH, MIX_WIDTH, D_MODEL), jnp.float32) * (MIX_WIDTH ** -0.5 * DEEPNORM_BETA)
    lam = 0.1 * nrm(ks[3], (DEPTH, 4, DIFF_DH), jnp.float32)
    subln_g = 1.0 + 0.01 * nrm(ks[4], (DEPTH, HEAD_DIM), jnp.float32)
    ln1_g = 1.0 + 0.01 * nrm(ks[5], (DEPTH, D_MODEL), jnp.float32)
    ln1_b = 0.01 * nrm(ks[6], (DEPTH, D_MODEL), jnp.float32)
    w_gate = nrm(ks[7], (DEPTH, D_MODEL, FF_DIM), jnp.float32) * D_MODEL ** -0.5
    w_up = nrm(ks[8], (DEPTH, D_MODEL, FF_DIM), jnp.float32) * D_MODEL ** -0.5
    w_down = nrm(ks[9], (DEPTH, FF_DIM, D_MODEL), jnp.float32) * (FF_DIM ** -0.5 * DEEPNORM_BETA)
    ln2_g = 1.0 + 0.01 * nrm(ks[10], (DEPTH, D_MODEL), jnp.float32)
    ln2_b = 0.01 * nrm(ks[11], (DEPTH, D_MODEL), jnp.float32)
    return {'x': x, 'w_in': w_in, 'w_o': w_o, 'lam': lam, 'subln_g': subln_g,
            'ln1_g': ln1_g, 'ln1_b': ln1_b, 'w_gate': w_gate, 'w_up': w_up,
            'w_down': w_down, 'ln2_g': ln2_g, 'ln2_b': ln2_b}


def reference(x, w_in, w_o, lam, subln_g, ln1_g, ln1_b, w_gate, w_up, w_down, ln2_g, ln2_b):
    for l in range(DEPTH):
        lambda_init = 0.8 - 0.6 * math.exp(-0.3 * l)
        mix = hybrid_mixer(x, w_in[l], w_o[l], lam[l], subln_g[l], lambda_init)
        x = layer_norm(DEEPNORM_ALPHA * x + mix, ln1_g[l], ln1_b[l])
        f = (jax.nn.silu(x @ w_gate[l]) * (x @ w_up[l])) @ w_down[l]
        x = layer_norm(DEEPNORM_ALPHA * x + f, ln2_g[l], ln2_b[l])
    return x
```

```python
import functools
import math

import jax
import jax.numpy as jnp
from jax import lax
from jax.experimental import pallas as pl
from jax.experimental.pallas import tpu as pltpu

F32 = jnp.float32
BF16 = jnp.bfloat16
I32 = jnp.int32

D_MODEL = 1024
DEPTH = 2
HEAD_DIM = 64
A_HEADS = 4
IDX_HEADS = 4
TOPK = 256
B_HEADS = 4
DIFF_DH = 32
C_HEADS = 8
C_DILATIONS = (1, 4, 16)
C_STEPS = 128
QB = 128
FF_DIM = 2816
ALPHA = (2 * DEPTH) ** 0.25
LN_EPS = 1e-5
NEG = -1e30

A_COLS = 1024
B_COLS = 768
C_COLS = 1536

VMEM_LIMIT = 56 * 1024 * 1024

INT_MIN = -2 ** 31
KEY_NEG_INF = (-8388608) ^ 0x7FFFFFFF


def _slopes(n):
    return [2.0 ** (-8.0 * (i + 1) / n) for i in range(n)]


def _nt_dot(a, b):
    return lax.dot_general(a, b, (((1,), (1,)), ((), ())), preferred_element_type=F32)


def _nn_dot(a, b):
    return jnp.dot(a, b, preferred_element_type=F32)


def _params(sem):
    return pltpu.CompilerParams(dimension_semantics=sem, vmem_limit_bytes=VMEM_LIMIT)


def _inproj_kernel(x_ref, w_ref, za_ref, zb_ref, zc_ref):
    x = x_ref[...].astype(BF16)
    za_ref[...] = _nn_dot(x, w_ref[:, 0:A_COLS])
    zb_ref[...] = _nn_dot(x, w_ref[:, A_COLS:A_COLS + B_COLS]).astype(BF16)
    zc_ref[...] = _nn_dot(x, w_ref[:, A_COLS + B_COLS:]).astype(BF16)


def _inproj(x2, w_pad, tm=512):
    n = x2.shape[0]
    ncols = A_COLS + B_COLS + C_COLS
    return pl.pallas_call(
        _inproj_kernel,
        grid=(n // tm,),
        in_specs=[pl.BlockSpec((tm, D_MODEL), lambda i: (i, 0)),
                  pl.BlockSpec((D_MODEL, ncols), lambda i: (0, 0))],
        out_specs=[pl.BlockSpec((tm, A_COLS), lambda i: (i, 0)),
                   pl.BlockSpec((tm, B_COLS), lambda i: (i, 0)),
                   pl.BlockSpec((tm, C_COLS), lambda i: (i, 0))],
        out_shape=[jax.ShapeDtypeStruct((n, A_COLS), F32),
                   jax.ShapeDtypeStruct((n, B_COLS), BF16),
                   jax.ShapeDtypeStruct((n, C_COLS), BF16)],
        compiler_params=_params(("parallel",)),
        name="inproj",
    )(x2, w_pad)


def _pad_w_in(w):
    def z(k):
        return jnp.zeros((D_MODEL, k), w.dtype)
    qa, ka, va = w[:, 0:256], w[:, 256:320], w[:, 320:384]
    qi, ki, wi = w[:, 384:640], w[:, 640:704], w[:, 704:708]
    rest = w[:, 708:]
    a = jnp.concatenate([qa, qi, ka, z(64), va, z(64), ki, z(64), wi, z(124)], axis=1)
    return jnp.concatenate([a, rest], axis=1).astype(BF16)


def _softmax_rows(s_ref, e_ref, linv_ref, row0, sk, bias_fn, scale=None, rows=16, nchunks=8):
    def body(rc, carry):
        r = pl.multiple_of(row0 + rc * rows, rows)
        x = s_ref[pl.ds(r, rows), 0:sk]
        if scale is not None:
            x = x * scale
        x = x + bias_fn(rc)
        m = jnp.max(x, axis=1, keepdims=True)
        e = jnp.exp(x - m)
        l = jnp.sum(e, axis=1, keepdims=True)
        e_ref[pl.ds(r, rows), 0:sk] = e.astype(BF16)
        linv_ref[pl.ds(r, rows), :] = jnp.broadcast_to(1.0 / l, (rows, 128))
        return carry
    lax.fori_loop(0, nchunks, body, 0)


def _attn_a_tile(i, zq_ref, ka_ref, va_ref, ki_ref, o_ref, s_ref, key_ref, mb_ref, e_ref, linv_ref, *, sk):
    t0 = i * QB
    zq = zq_ref[0]
    ki = ki_ref[0, 0:sk, 0:64].astype(BF16)
    qi = zq[:, 256:512] * 0.125
    qi_stack = jnp.concatenate([qi[:, h * 64:(h + 1) * 64] for h in range(IDX_HEADS)], axis=0).astype(BF16)
    s_ref[:, 0:sk] = _nt_dot(qi_stack, ki)
    wi = zq[:, 896:900] * 0.5
    row_t = t0 + lax.broadcasted_iota(I32, (QB, sk), 0)
    col = lax.broadcasted_iota(I32, (QB, sk), 1)
    causal = col <= row_t
    score = jnp.zeros((QB, sk), F32)
    for h in range(IDX_HEADS):
        score = score + jnp.maximum(s_ref[h * QB:(h + 1) * QB, 0:sk], 0.0) * wi[:, h:h + 1]
    score = jnp.where(causal, score, -jnp.inf)
    bits = lax.bitcast_convert_type(score, I32)
    key_ref[:, 0:sk] = jnp.where(bits < 0, bits ^ 0x7FFFFFFF, bits)

    def count_ge(cand):
        return jnp.sum(jnp.where(key_ref[:, 0:sk] >= cand, 1.0, 0.0), axis=1, keepdims=True)

    kf = float(TOPK)
    thr0 = jnp.where(count_ge(jnp.zeros((QB, 1), I32)) >= kf, 0, INT_MIN).astype(I32)

    def search(it, thr):
        cand = thr | lax.shift_left(jnp.int32(1), 30 - it)
        return jnp.where(count_ge(cand) >= kf, cand, thr)

    thr = lax.fori_loop(0, 31, search, thr0)

    key = key_ref[:, 0:sk]
    n_gt = jnp.sum(jnp.where(key > thr, 1.0, 0.0), axis=1, keepdims=True)
    n_eq = jnp.sum(jnp.where(key == thr, 1.0, 0.0), axis=1, keepdims=True)
    need = kf - n_gt
    mb_ref[:, 0:sk] = jnp.where(key >= thr, jnp.where(causal, 0.0, NEG), NEG)
    ambiguous = jnp.where((n_eq > need) & (thr != KEY_NEG_INF), 1.0, 0.0)

    @pl.when(jnp.max(ambiguous) > 0.0)
    def _():
        r_i = lax.broadcasted_iota(I32, (128, 128), 0)
        c_i = lax.broadcasted_iota(I32, (128, 128), 1)
        upper = jnp.where(r_i < c_i, 1.0, 0.0).astype(BF16)
        base = jnp.zeros((QB, 1), F32)
        for c in range(sk // 128):
            kc = key_ref[:, c * 128:(c + 1) * 128]
            eq = kc == thr
            eqf = jnp.where(eq, 1.0, 0.0)
            rank = base + _nn_dot(eqf.astype(BF16), upper)
            sel = (kc > thr) | (eq & (rank < need))
            cz = causal[:, c * 128:(c + 1) * 128]
            mb_ref[:, c * 128:(c + 1) * 128] = jnp.where(sel & cz, 0.0, NEG)
            base = base + jnp.sum(eqf, axis=1, keepdims=True)

    ka = ka_ref[0, 0:sk, 0:64].astype(BF16)
    va = va_ref[0, 0:sk, 0:64].astype(BF16)
    qa = zq[:, 0:256] * 0.125
    qa_stack = jnp.concatenate([qa[:, h * 64:(h + 1) * 64] for h in range(A_HEADS)], axis=0).astype(BF16)
    s_ref[:, 0:sk] = _nt_dot(qa_stack, ka)
    colf = lax.broadcasted_iota(I32, (1, sk), 1).astype(F32)
    for h, slope in enumerate(_slopes(A_HEADS)):
        cb = slope * colf
        _softmax_rows(s_ref, e_ref, linv_ref, h * QB, sk,
                      lambda rc, cb=cb: cb + mb_ref[pl.ds(pl.multiple_of(rc * 16, 16), 16), 0:sk])
    o = _nn_dot(e_ref[:, 0:sk], va) * linv_ref[:, 0:64]
    o_ref[0] = jnp.concatenate([o[h * QB:(h + 1) * QB] for h in range(A_HEADS)], axis=1).astype(o_ref.dtype)


def _attn_a_kernel(zq_ref, ka_ref, va_ref, ki_ref, o_ref, s_ref, key_ref, mb_ref, e_ref, linv_ref, *, seq):
    i = pl.program_id(1)
    ncls = 4
    per = seq // QB // ncls
    for c in range(ncls):
        @pl.when(i // per == c)
        def _(c=c):
            _attn_a_tile(i, zq_ref, ka_ref, va_ref, ki_ref, o_ref, s_ref, key_ref, mb_ref, e_ref, linv_ref,
                         sk=(c + 1) * per * QB)


def _attn_a(za3):
    b, s, _ = za3.shape
    kv_spec = lambda cb: pl.BlockSpec((1, s, 128), lambda bi, i, cb=cb: (bi, 0, cb))
    return pl.pallas_call(
        functools.partial(_attn_a_kernel, seq=s),
        grid=(b, s // QB),
        in_specs=[pl.BlockSpec((1, QB, A_COLS), lambda bi, i: (bi, i, 0)),
                  kv_spec(4), kv_spec(5), kv_spec(6)],
        out_specs=pl.BlockSpec((1, QB, 256), lambda bi, i: (bi, i, 0)),
        out_shape=jax.ShapeDtypeStruct((b, s, 256), BF16),
        scratch_shapes=[pltpu.VMEM((4 * QB, s), F32), pltpu.VMEM((QB, s), I32), pltpu.VMEM((QB, s), F32),
                        pltpu.VMEM((4 * QB, s), BF16), pltpu.VMEM((4 * QB, 128), F32)],
        compiler_params=_params(("parallel", "arbitrary")),
        name="attn_a",
    )(za3, za3, za3, za3)


def _attn_b_tile(i, q_ref, k_ref, v_ref, lam_ref, g_ref, o_ref, s_ref, cm_ref, e_ref, linv_ref, *, sk, lambda_init):
    t0 = i * QB
    row_t = t0 + lax.broadcasted_iota(I32, (QB, sk), 0)
    col = lax.broadcasted_iota(I32, (QB, sk), 1)
    cm_ref[:, 0:sk] = jnp.where(col <= row_t, 0.0, NEG)
    colf = lax.broadcasted_iota(I32, (1, sk), 1).astype(F32)
    lam = lam_ref[...]
    lam_full = (jnp.exp(jnp.sum(lam[0:1] * lam[1:2], axis=1, keepdims=True))
                - jnp.exp(jnp.sum(lam[2:3] * lam[3:4], axis=1, keepdims=True)) + lambda_init)
    lane = lax.broadcasted_iota(I32, (QB, 128), 1)
    slopes = _slopes(B_HEADS)
    scale = DIFF_DH ** -0.5
    for p in range(B_HEADS // 2):
        qp = q_ref[0, :, p * 128:(p + 1) * 128]
        kp = k_ref[0, 0:sk, p * 128:(p + 1) * 128]
        vp = v_ref[0, 0:sk, p * 128:(p + 1) * 128]
        q_stack = jnp.concatenate(
            [jnp.where((lane >= g * 32) & (lane < (g + 1) * 32), qp, jnp.zeros_like(qp)) for g in range(4)], axis=0)
        s_ref[:, 0:sk] = _nt_dot(q_stack, kp)
        for g in range(4):
            cb = slopes[2 * p + g // 2] * colf
            _softmax_rows(s_ref, e_ref, linv_ref, g * QB, sk,
                          lambda rc, cb=cb: cb + cm_ref[pl.ds(pl.multiple_of(rc * 16, 16), 16), 0:sk],
                          scale=scale)
        r = _nn_dot(e_ref[:, 0:sk], vp) * linv_ref[...]
        o0 = r[0:QB] - lam_full * r[QB:2 * QB]
        o1 = r[2 * QB:3 * QB] - lam_full * r[3 * QB:4 * QB]
        first = lane < 64
        o = jnp.where(first, o0, o1)
        sq = o * o
        ms0 = jnp.sum(jnp.where(first, sq, 0.0), axis=1, keepdims=True)
        ms1 = jnp.sum(jnp.where(first, 0.0, sq), axis=1, keepdims=True)
        ms = jnp.where(first, ms0, ms1) * (1.0 / HEAD_DIM)
        o = o * lax.rsqrt(ms + LN_EPS) * g_ref[...] * (1.0 - lambda_init)
        o_ref[0, :, p * 128:(p + 1) * 128] = o.astype(o_ref.dtype)


def _attn_b_kernel(q_ref, k_ref, v_ref, lam_ref, g_ref, o_ref, s_ref, cm_ref, e_ref, linv_ref, *, seq, lambda_init):
    i = pl.program_id(1)
    ncls = 4
    per = seq // QB // ncls
    for c in range(ncls):
        @pl.when(i // per == c)
        def _(c=c):
            _attn_b_tile(i, q_ref, k_ref, v_ref, lam_ref, g_ref, o_ref, s_ref, cm_ref, e_ref, linv_ref,
                         sk=(c + 1) * per * QB, lambda_init=lambda_init)


def _attn_b(zb3, lam, g_pair, lambda_init):
    b, s, _ = zb3.shape
    return pl.pallas_call(
        functools.partial(_attn_b_kernel, seq=s, lambda_init=lambda_init),
        grid=(b, s // QB),
        in_specs=[pl.BlockSpec((1, QB, 256), lambda bi, i: (bi, i, 0)),
                  pl.BlockSpec((1, s, 256), lambda bi, i: (bi, 0, 1)),
                  pl.BlockSpec((1, s, 256), lambda bi, i: (bi, 0, 2)),
                  pl.BlockSpec((4, DIFF_DH), lambda bi, i: (0, 0)),
                  pl.BlockSpec((1, 128), lambda bi, i: (0, 0))],
        out_specs=pl.BlockSpec((1, QB, 256), lambda bi, i: (bi, i, 0)),
        out_shape=jax.ShapeDtypeStruct((b, s, 256), BF16),
        scratch_shapes=[pltpu.VMEM((4 * QB, s), F32), pltpu.VMEM((QB, s), F32),
                        pltpu.VMEM((4 * QB, s), BF16), pltpu.VMEM((4 * QB, 128), F32)],
        compiler_params=_params(("parallel", "arbitrary")),
        name="attn_b",
    )(zb3, zb3, zb3, lam, g_pair)


def _c_bias_tables(dilation):
    qi = jnp.arange(128)[:, None]
    kj = jnp.arange(256)[None, :]
    slopes = jnp.asarray(_slopes(C_HEADS), F32)[:, None, None]
    d_first = qi - kj
    d_mid = qi + 128 - kj
    def tab(delta):
        valid = (delta >= 0) & (delta <= C_STEPS)
        return jnp.where(valid[None], -slopes * (delta * dilation).astype(F32)[None], NEG)
    return jnp.stack([tab(d_first), tab(d_mid)]).astype(F32)


def _attn_c_kernel(z_ref, bias_ref, o_ref, lse_ref, *, ls, groups):
    nb = ls // 128
    lane = lax.broadcasted_iota(I32, (128, 128), 1)
    first = lane < 64

    def tile(n, o, p):
        base = o * C_COLS
        if nb == 1:
            q0, k0, kw, bsel = 0, 0, 128, 0
        else:
            q0 = pl.multiple_of(n * 128, 128)
            k0 = pl.multiple_of(jnp.maximum(n - 1, 0) * 128, 128)
            kw, bsel = 256, jnp.minimum(n, 1)
        qp = z_ref[0, pl.ds(q0, 128), base + p * 128:base + (p + 1) * 128]
        qp = qp * jnp.asarray(0.125, BF16)
        kp = z_ref[0, pl.ds(k0, kw), base + 512 + p * 128:base + 512 + (p + 1) * 128]
        vp = z_ref[0, pl.ds(k0, kw), base + 1024 + p * 128:base + 1024 + (p + 1) * 128]
        zero = jnp.zeros_like(qp)
        q_stack = jnp.concatenate([jnp.where(first, qp, zero), jnp.where(first, zero, qp)], axis=0)
        s = _nt_dot(q_stack, kp)
        es, lses, linvs = [], [], []
        for hh in range(2):
            x = s[hh * 128:(hh + 1) * 128] + bias_ref[bsel, 2 * p + hh, :, 0:kw]
            m = jnp.max(x, axis=1, keepdims=True)
            e = jnp.exp(x - m)
            l = jnp.sum(e, axis=1, keepdims=True)
            es.append(e.astype(BF16))
            lses.append(m + jnp.log(l))
            linvs.append(1.0 / l)
        r = _nn_dot(jnp.concatenate(es, axis=0), vp)
        out = jnp.where(first, r[0:128] * linvs[0], r[128:256] * linvs[1])
        lse = jnp.where(first, lses[0], lses[1])
        o_ref[0, pl.ds(q0, 128), o * 512 + p * 128:o * 512 + (p + 1) * 128] = out.astype(o_ref.dtype)
        lse_ref[0, pl.ds(q0, 128), o * 512 + p * 128:o * 512 + (p + 1) * 128] = lse

    for o in range(groups):
        for p in range(C_HEADS // 2):
            if nb == 1:
                tile(0, o, p)
            else:
                def body(n, carry, o=o, p=p):
                    tile(n, o, p)
                    return carry
                lax.fori_loop(0, nb, body, 0)


def _attn_c(zc3, dilation, groups):
    b, s, _ = zc3.shape
    ls = s // dilation
    zv = zc3.reshape(b, ls, dilation * C_COLS)
    bias = _c_bias_tables(dilation)
    out, lse = pl.pallas_call(
        functools.partial(_attn_c_kernel, ls=ls, groups=groups),
        grid=(b, dilation // groups),
        in_specs=[pl.BlockSpec((1, ls, groups * C_COLS), lambda bi, j: (bi, 0, j)),
                  pl.BlockSpec((2, C_HEADS, 128, 256), lambda bi, j: (0, 0, 0, 0))],
        out_specs=[pl.BlockSpec((1, ls, groups * 512), lambda bi, j: (bi, 0, j)),
                   pl.BlockSpec((1, ls, groups * 512), lambda bi, j: (bi, 0, j))],
        out_shape=[jax.ShapeDtypeStruct((b, ls, dilation * 512), F32),
                   jax.ShapeDtypeStruct((b, ls, dilation * 512), F32)],
        compiler_params=_params(("parallel", "arbitrary")),
        name=f"attn_c_d{dilation}",
    )(zv, bias)
    return out.reshape(b * s, 512), lse.reshape(b * s, 512)


def _layer_norm(y, g, b):
    mu = jnp.mean(y, axis=-1, keepdims=True)
    d = y - mu
    var = jnp.mean(d * d, axis=-1, keepdims=True)
    return d * lax.rsqrt(var + LN_EPS) * g + b


def _mix_kernel(x_ref, oa_ref, ob_ref, c0_ref, c1_ref, c2_ref, l0_ref, l1_ref, l2_ref, w_ref, g_ref, b_ref, y_ref):
    l0, l1, l2 = l0_ref[...], l1_ref[...], l2_ref[...]
    m = jnp.maximum(jnp.maximum(l0, l1), l2)
    w0, w1, w2 = jnp.exp(l0 - m), jnp.exp(l1 - m), jnp.exp(l2 - m)
    oc = (w0 * c0_ref[...] + w1 * c1_ref[...] + w2 * c2_ref[...]) / (w0 + w1 + w2)
    mix = (_nn_dot(oa_ref[...], w_ref[0:256, :]) + _nn_dot(ob_ref[...], w_ref[256:512, :])
           + _nn_dot(oc.astype(BF16), w_ref[512:1024, :]))
    y_ref[...] = _layer_norm(ALPHA * x_ref[...] + mix, g_ref[...], b_ref[...])


def _mix(x2, oa, ob, cs, ls, w_o, g, b, tm=512):
    n = x2.shape[0]
    row = lambda w: pl.BlockSpec((tm, w), lambda i: (i, 0))
    const = lambda r, c: pl.BlockSpec((r, c), lambda i: (0, 0))
    return pl.pallas_call(
        _mix_kernel,
        grid=(n // tm,),
        in_specs=[row(D_MODEL), row(256), row(256)] + [row(512)] * 6
                 + [const(D_MODEL, D_MODEL), const(1, D_MODEL), const(1, D_MODEL)],
        out_specs=row(D_MODEL),
        out_shape=jax.ShapeDtypeStruct((n, D_MODEL), F32),
        compiler_params=_params(("parallel",)),
        name="mix_wo_ln",
    )(x2, oa, ob, *cs, *ls, w_o, g, b)


FF_CHUNKS = ((0, 768), (768, 768), (1536, 768), (2304, 512))


def _ffn_kernel(x_ref, wg_ref, wu_ref, wd_ref, g_ref, b_ref, y_ref):
    x = x_ref[...]
    xb = x.astype(BF16)
    acc = ALPHA * x
    for start, width in FF_CHUNKS:
        gate = _nn_dot(xb, wg_ref[:, start:start + width])
        up = _nn_dot(xb, wu_ref[:, start:start + width])
        h = (gate * jax.nn.sigmoid(gate) * up).astype(BF16)
        acc = acc + _nn_dot(h, wd_ref[start:start + width, :])
    y_ref[...] = _layer_norm(acc, g_ref[...], b_ref[...])


def _ffn(x2, wg, wu, wd, g, b, tm=512):
    n = x2.shape[0]
    single = pl.Buffered(1)
    return pl.pallas_call(
        _ffn_kernel,
        grid=(n // tm,),
        in_specs=[pl.BlockSpec((tm, D_MODEL), lambda i: (i, 0)),
                  pl.BlockSpec((D_MODEL, FF_DIM), lambda i: (0, 0), pipeline_mode=single),
                  pl.BlockSpec((D_MODEL, FF_DIM), lambda i: (0, 0), pipeline_mode=single),
                  pl.BlockSpec((FF_DIM, D_MODEL), lambda i: (0, 0), pipeline_mode=single),
                  pl.BlockSpec((1, D_MODEL), lambda i: (0, 0)),
                  pl.BlockSpec((1, D_MODEL), lambda i: (0, 0))],
        out_specs=pl.BlockSpec((tm, D_MODEL), lambda i: (i, 0)),
        out_shape=jax.ShapeDtypeStruct((n, D_MODEL), F32),
        compiler_params=_params(("parallel",)),
        name="ffn_ln",
    )(x2, wg, wu, wd, g, b)


C_GROUPS = {1: 1, 4: 4, 16: 4}


def kernel(x, w_in, w_o, lam, subln_g, ln1_g, ln1_b, w_gate, w_up, w_down, ln2_g, ln2_b):
    b, s, d = x.shape
    x2 = x.reshape(b * s, d)
    for l in range(DEPTH):
        lambda_init = 0.8 - 0.6 * math.exp(-0.3 * l)
        za, zb, zc = _inproj(x2, _pad_w_in(w_in[l]))
        oa = _attn_a(za.reshape(b, s, A_COLS)).reshape(b * s, 256)
        g_pair = jnp.concatenate([subln_g[l], subln_g[l]]).reshape(1, 128).astype(F32)
        ob = _attn_b(zb.reshape(b, s, B_COLS), lam[l].astype(F32), g_pair, lambda_init).reshape(b * s, 256)
        zc3 = zc.reshape(b, s, C_COLS)
        cs, ls = zip(*[_attn_c(zc3, dil, C_GROUPS[dil]) for dil in C_DILATIONS])
        x2 = _mix(x2, oa, ob, cs, ls, w_o[l].astype(BF16),
                  ln1_g[l].reshape(1, d), ln1_b[l].reshape(1, d))
        x2 = _ffn(x2, w_gate[l].astype(BF16), w_up[l].astype(BF16), w_down[l].astype(BF16),
                  ln2_g[l].reshape(1, d), ln2_b[l].reshape(1, d))
    return x2.reshape(b, s, d)
```

```python
import functools
import math

import jax
import jax.numpy as jnp
from jax import lax
from jax.experimental import pallas as pl
from jax.experimental.pallas import tpu as pltpu

F32 = jnp.float32
BF16 = jnp.bfloat16
I32 = jnp.int32

D_MODEL = 1024
DEPTH = 2
HEAD_DIM = 64
A_HEADS = 4
IDX_HEADS = 4
TOPK = 256
B_HEADS = 4
DIFF_DH = 32
C_HEADS = 8
C_DILATIONS = (1, 4, 16)
C_STEPS = 128
QB = 128
N_CLASSES = 4
FF_DIM = 2816
ALPHA = (2 * DEPTH) ** 0.25
LN_EPS = 1e-5
NEG = -1e30

A_COLS = 1024
B_COLS = 768
C_COLS = 1536

VMEM_LIMIT = 56 * 1024 * 1024

INT_MIN = -2 ** 31
KEY_NEG_INF = (-8388608) ^ 0x7FFFFFFF


def _slopes(n):
    return [2.0 ** (-8.0 * (i + 1) / n) for i in range(n)]


def _nt_dot(a, b):
    return lax.dot_general(a, b, (((1,), (1,)), ((), ())), preferred_element_type=F32)


def _nn_dot(a, b):
    return jnp.dot(a, b, preferred_element_type=F32)


def _params(sem):
    return pltpu.CompilerParams(dimension_semantics=sem, vmem_limit_bytes=VMEM_LIMIT)


def _position_columns(rows, width):
    r = lax.broadcasted_iota(I32, (rows, width), 0)
    lane = lax.broadcasted_iota(I32, (rows, width), 1)
    return jnp.where(lane == 0, r & 127, jnp.where(lane == 1, r >> 7, 0)).astype(F32)


def _slope_columns(slope, rows, width):
    lane = lax.broadcasted_iota(I32, (rows, width), 1)
    return jnp.where(lane == 0, slope, jnp.where(lane == 1, slope * 128.0, 0.0)).astype(F32)


def _softmax_map(s_ref, e_ref, linv_ref, row0, sk, mask, mask_cols):
    rows = slice(row0, row0 + QB)
    lo = sk - mask_cols
    xb = s_ref[rows, lo:sk] + mask
    m = jnp.max(xb, axis=1, keepdims=True)
    if lo > 0:
        xa = s_ref[rows, 0:lo]
        m = jnp.maximum(m, jnp.max(xa, axis=1, keepdims=True))
        ea = jnp.exp(xa - m)
        e_ref[rows, 0:lo] = ea.astype(BF16)
    eb = jnp.exp(xb - m)
    e_ref[rows, lo:sk] = eb.astype(BF16)
    l = jnp.sum(eb, axis=1, keepdims=True)
    if lo > 0:
        l = l + jnp.sum(ea, axis=1, keepdims=True)
    linv_ref[rows, :] = jnp.broadcast_to(1.0 / l, (QB, 128))


def _inproj_kernel(x_ref, w_ref, za_ref, zb_ref, zc1_ref, zc4_ref, zc16_ref, acc_ref, *, tm):
    x = x_ref[...].astype(BF16)
    za_ref[...] = _nn_dot(x, w_ref[:, 0:A_COLS])
    zb = _nn_dot(x, w_ref[:, A_COLS:A_COLS + B_COLS])
    lane = lax.broadcasted_iota(I32, (1, B_COLS), 1)
    zb_ref[...] = (zb * jnp.where(lane < 256, DIFF_DH ** -0.5, 1.0)).astype(BF16)
    zc = _nn_dot(x, w_ref[:, A_COLS + B_COLS:])
    lane = lax.broadcasted_iota(I32, (1, C_COLS), 1)
    zc = zc * jnp.where(lane < 512, HEAD_DIM ** -0.5, 1.0)
    zc1_ref[...] = zc.astype(BF16)
    for c in range(C_COLS // 128):
        acc_ref[c] = zc[:, c * 128:(c + 1) * 128]
    for dst, r in ((zc4_ref, 4), (zc16_ref, 16)):
        for o in range(r):
            for c in range(C_COLS // 128):
                dst[0, o, :, c * 128:(c + 1) * 128] = acc_ref[c, pl.ds(o, tm // r, stride=r), :].astype(BF16)


def _inproj(x2, w_pad, b, s, tm=512):
    n = x2.shape[0]
    ncols = A_COLS + B_COLS + C_COLS
    tiles = s // tm
    row = lambda w: pl.BlockSpec((tm, w), lambda i: (i, 0))
    cls = lambda r: pl.BlockSpec((1, r, tm // r, C_COLS), lambda i: (i // tiles, 0, i % tiles, 0))
    return pl.pallas_call(
        functools.partial(_inproj_kernel, tm=tm),
        grid=(n // tm,),
        in_specs=[row(D_MODEL), pl.BlockSpec((D_MODEL, ncols), lambda i: (0, 0))],
        out_specs=[row(A_COLS), row(B_COLS), row(C_COLS), cls(4), cls(16)],
        out_shape=[jax.ShapeDtypeStruct((n, A_COLS), F32),
                   jax.ShapeDtypeStruct((n, B_COLS), BF16),
                   jax.ShapeDtypeStruct((n, C_COLS), BF16),
                   jax.ShapeDtypeStruct((b, 4, s // 4, C_COLS), BF16),
                   jax.ShapeDtypeStruct((b, 16, s // 16, C_COLS), BF16)],
        scratch_shapes=[pltpu.VMEM((C_COLS // 128, tm, 128), F32)],
        compiler_params=_params(("parallel",)),
        name="inproj",
    )(x2, w_pad)


def _pad_w_in(w):
    def z(k):
        return jnp.zeros((D_MODEL, k), w.dtype)
    qa, ka, va = w[:, 0:256], w[:, 256:320], w[:, 320:384]
    qi, ki, wi = w[:, 384:640], w[:, 640:704], w[:, 704:708]
    rest = w[:, 708:]
    a = jnp.concatenate([qa, qi, ka, z(64), va, z(64), ki, z(64), wi, z(124)], axis=1)
    return jnp.concatenate([a, rest], axis=1).astype(BF16)


def _attn_a_tile(i, zq_ref, o_ref, kax_ref, vab_ref, kib_ref, s_ref, key_ref, mb_ref, e_ref, linv_ref, *, sk):
    t0 = i * QB
    zq = zq_ref[0]
    qi = zq[:, 256:512] * 0.125
    zeros64 = jnp.zeros((QB, 64), F32)
    qi_stack = jnp.concatenate(
        [jnp.concatenate([qi[:, h * 64:(h + 1) * 64], zeros64], axis=1) for h in range(IDX_HEADS)], axis=0)
    s_ref[:, 0:sk] = _nt_dot(qi_stack.astype(BF16), kib_ref[0:sk, :])
    wi = zq[:, 896:900] * 0.5
    row_t = t0 + lax.broadcasted_iota(I32, (QB, sk), 0)
    col = lax.broadcasted_iota(I32, (QB, sk), 1)
    causal = col <= row_t
    score = jnp.zeros((QB, sk), F32)
    for h in range(IDX_HEADS):
        score = score + jnp.maximum(s_ref[h * QB:(h + 1) * QB, 0:sk], 0.0) * wi[:, h:h + 1]
    score = jnp.where(causal, score, -jnp.inf)
    bits = lax.bitcast_convert_type(score, I32)
    key_ref[:, 0:sk] = jnp.where(bits < 0, bits ^ 0x7FFFFFFF, bits)

    def count_ge(cand):
        return jnp.sum(jnp.where(key_ref[:, 0:sk] >= cand, 1.0, 0.0), axis=1, keepdims=True)

    kf = float(TOPK)
    thr0 = jnp.where(count_ge(jnp.zeros((QB, 1), I32)) >= kf, 0, INT_MIN).astype(I32)

    def search(it, thr):
        cand = thr | lax.shift_left(jnp.int32(1), 30 - it)
        return jnp.where(count_ge(cand) >= kf, cand, thr)

    thr = lax.fori_loop(0, 31, search, thr0)

    key = key_ref[:, 0:sk]
    n_gt = jnp.sum(jnp.where(key > thr, 1.0, 0.0), axis=1, keepdims=True)
    n_eq = jnp.sum(jnp.where(key == thr, 1.0, 0.0), axis=1, keepdims=True)
    need = kf - n_gt
    mb_ref[:, 0:sk] = jnp.where(key >= thr, jnp.where(causal, 0.0, NEG), NEG)
    ambiguous = jnp.where((n_eq > need) & (thr != KEY_NEG_INF), 1.0, 0.0)

    @pl.when(jnp.max(ambiguous) > 0.0)
    def _():
        r_i = lax.broadcasted_iota(I32, (128, 128), 0)
        c_i = lax.broadcasted_iota(I32, (128, 128), 1)
        upper = jnp.where(r_i < c_i, 1.0, 0.0).astype(BF16)
        base = jnp.zeros((QB, 1), F32)
        for c in range(sk // 128):
            kc = key_ref[:, c * 128:(c + 1) * 128]
            eq = kc == thr
            eqf = jnp.where(eq, 1.0, 0.0)
            rank = base + _nn_dot(eqf.astype(BF16), upper)
            sel = (kc > thr) | (eq & (rank < need))
            cz = causal[:, c * 128:(c + 1) * 128]
            mb_ref[:, c * 128:(c + 1) * 128] = jnp.where(sel & cz, 0.0, NEG)
            base = base + jnp.sum(eqf, axis=1, keepdims=True)

    qa = zq[:, 0:256] * 0.125
    qa_stack = jnp.concatenate(
        [jnp.concatenate([qa[:, h * 64:(h + 1) * 64], _slope_columns(slope, QB, 64)], axis=1)
         for h, slope in enumerate(_slopes(A_HEADS))], axis=0)
    s_ref[:, 0:sk] = _nt_dot(qa_stack.astype(BF16), kax_ref[0:sk, :])
    mb = mb_ref[:, 0:sk]
    for h in range(A_HEADS):
        _softmax_map(s_ref, e_ref, linv_ref, h * QB, sk, mb, sk)
    o = _nn_dot(e_ref[:, 0:sk], vab_ref[0:sk, :]) * linv_ref[...]
    o_ref[0] = jnp.concatenate([o[h * QB:(h + 1) * QB, 0:64] for h in range(A_HEADS)], axis=1).astype(o_ref.dtype)


def _attn_a_kernel(zq_ref, ka_ref, va_ref, ki_ref, o_ref,
                   kax_ref, vab_ref, kib_ref, s_ref, key_ref, mb_ref, e_ref, linv_ref, *, seq):
    i = pl.program_id(1)

    @pl.when(i == 0)
    def _():
        kax_ref[...] = (ka_ref[0] + jnp.concatenate(
            [jnp.zeros((seq, 64), F32), _position_columns(seq, 64)], axis=1)).astype(BF16)
        vab_ref[...] = va_ref[0].astype(BF16)
        kib_ref[...] = ki_ref[0].astype(BF16)

    per = seq // QB // N_CLASSES
    for c in range(N_CLASSES):
        @pl.when(i // per == c)
        def _(c=c):
            _attn_a_tile(i, zq_ref, o_ref, kax_ref, vab_ref, kib_ref, s_ref, key_ref, mb_ref, e_ref, linv_ref,
                         sk=(c + 1) * per * QB)


def _attn_a(za3):
    b, s, _ = za3.shape
    kv_spec = lambda cb: pl.BlockSpec((1, s, 128), lambda bi, i, cb=cb: (bi, 0, cb))
    return pl.pallas_call(
        functools.partial(_attn_a_kernel, seq=s),
        grid=(b, s // QB),
        in_specs=[pl.BlockSpec((1, QB, A_COLS), lambda bi, i: (bi, i, 0)),
                  kv_spec(4), kv_spec(5), kv_spec(6)],
        out_specs=pl.BlockSpec((1, QB, 256), lambda bi, i: (bi, i, 0)),
        out_shape=jax.ShapeDtypeStruct((b, s, 256), BF16),
        scratch_shapes=[pltpu.VMEM((s, 128), BF16), pltpu.VMEM((s, 128), BF16), pltpu.VMEM((s, 128), BF16),
                        pltpu.VMEM((4 * QB, s), F32), pltpu.VMEM((QB, s), I32), pltpu.VMEM((QB, s), F32),
                        pltpu.VMEM((4 * QB, s), BF16), pltpu.VMEM((4 * QB, 128), F32)],
        compiler_params=_params(("parallel", "arbitrary")),
        name="attn_a",
    )(za3, za3, za3, za3)


def _attn_b_tile(i, q_ref, v_ref, lam_ref, g_ref, o_ref, kx_ref, s_ref, e_ref, linv_ref, *, sk, per, lambda_init):
    mcols = per * QB
    row_t = (i % per) * QB + lax.broadcasted_iota(I32, (QB, mcols), 0)
    col = lax.broadcasted_iota(I32, (QB, mcols), 1)
    cm = jnp.where(col <= row_t, 0.0, NEG)
    lam = lam_ref[...]
    lam_full = (jnp.exp(jnp.sum(lam[0:1] * lam[1:2], axis=1, keepdims=True))
                - jnp.exp(jnp.sum(lam[2:3] * lam[3:4], axis=1, keepdims=True)) + lambda_init)
    lane = lax.broadcasted_iota(I32, (QB, 128), 1)
    slopes = _slopes(B_HEADS)
    for p in range(B_HEADS // 2):
        qp = q_ref[0, :, p * 128:(p + 1) * 128]
        vp = v_ref[0, 0:sk, p * 128:(p + 1) * 128]
        q_stack = jnp.concatenate(
            [jnp.concatenate([jnp.where((lane >= g * 32) & (lane < (g + 1) * 32), qp, jnp.zeros_like(qp)),
                              _slope_columns(slopes[2 * p + g // 2], QB, 128).astype(BF16)], axis=1)
             for g in range(4)], axis=0)
        s_ref[:, 0:sk] = _nt_dot(q_stack, kx_ref[p, 0:sk, :])
        for g in range(4):
            _softmax_map(s_ref, e_ref, linv_ref, g * QB, sk, cm, mcols)
        r = _nn_dot(e_ref[:, 0:sk], vp) * linv_ref[...]
        o0 = r[0:QB] - lam_full * r[QB:2 * QB]
        o1 = r[2 * QB:3 * QB] - lam_full * r[3 * QB:4 * QB]
        first = lane < 64
        o = jnp.where(first, o0, o1)
        sq = o * o
        ms0 = jnp.sum(jnp.where(first, sq, 0.0), axis=1, keepdims=True)
        ms1 = jnp.sum(jnp.where(first, 0.0, sq), axis=1, keepdims=True)
        ms = jnp.where(first, ms0, ms1) * (1.0 / HEAD_DIM)
        o = o * lax.rsqrt(ms + LN_EPS) * g_ref[...] * (1.0 - lambda_init)
        o_ref[0, :, p * 128:(p + 1) * 128] = o.astype(o_ref.dtype)


def _attn_b_kernel(q_ref, k_ref, v_ref, lam_ref, g_ref, o_ref, kx_ref, s_ref, e_ref, linv_ref, *, seq, lambda_init):
    i = pl.program_id(1)

    @pl.when(i == 0)
    def _():
        pos = _position_columns(seq, 128).astype(BF16)
        for p in range(B_HEADS // 2):
            kx_ref[p, :, 0:128] = k_ref[0, :, p * 128:(p + 1) * 128]
            kx_ref[p, :, 128:256] = pos

    per = seq // QB // N_CLASSES
    for c in range(N_CLASSES):
        @pl.when(i // per == c)
        def _(c=c):
            _attn_b_tile(i, q_ref, v_ref, lam_ref, g_ref, o_ref, kx_ref, s_ref, e_ref, linv_ref,
                         sk=(c + 1) * per * QB, per=per, lambda_init=lambda_init)


def _attn_b(zb3, lam, g_pair, lambda_init):
    b, s, _ = zb3.shape
    return pl.pallas_call(
        functools.partial(_attn_b_kernel, seq=s, lambda_init=lambda_init),
        grid=(b, s // QB),
        in_specs=[pl.BlockSpec((1, QB, 256), lambda bi, i: (bi, i, 0)),
                  pl.BlockSpec((1, s, 256), lambda bi, i: (bi, 0, 1)),
                  pl.BlockSpec((1, s, 256), lambda bi, i: (bi, 0, 2)),
                  pl.BlockSpec((4, DIFF_DH), lambda bi, i: (0, 0)),
                  pl.BlockSpec((1, 128), lambda bi, i: (0, 0))],
        out_specs=pl.BlockSpec((1, QB, 256), lambda bi, i: (bi, i, 0)),
        out_shape=jax.ShapeDtypeStruct((b, s, 256), BF16),
        scratch_shapes=[pltpu.VMEM((B_HEADS // 2, s, 256), BF16), pltpu.VMEM((4 * QB, s), F32),
                        pltpu.VMEM((4 * QB, s), BF16), pltpu.VMEM((4 * QB, 128), F32)],
        compiler_params=_params(("parallel", "arbitrary")),
        name="attn_b",
    )(zb3, zb3, zb3, lam, g_pair)


def _c_bias_tables(dilation):
    qi = jnp.arange(128)[:, None]
    kj = jnp.arange(256)[None, :]
    slopes = jnp.asarray(_slopes(C_HEADS), F32)[:, None, None]
    d_first = qi - kj
    d_mid = qi + 128 - kj
    def tab(delta):
        valid = (delta >= 0) & (delta <= C_STEPS)
        return jnp.where(valid[None], -slopes * (delta * dilation).astype(F32)[None], NEG)
    return jnp.stack([tab(d_first), tab(d_mid)]).astype(F32)


def _attn_c_kernel(z_ref, bias_ref, o_ref, lse_ref, *, ls, groups):
    nb = ls // 128
    lane = lax.broadcasted_iota(I32, (128, 128), 1)
    first = lane < 64

    def tile(n, o, p):
        if nb == 1:
            q0, k0, kw, bsel = 0, 0, 128, 0
        else:
            q0 = pl.multiple_of(n * 128, 128)
            k0 = pl.multiple_of(jnp.maximum(n - 1, 0) * 128, 128)
            kw, bsel = 256, jnp.minimum(n, 1)
        qp = z_ref[0, o, pl.ds(q0, 128), p * 128:(p + 1) * 128]
        kp = z_ref[0, o, pl.ds(k0, kw), 512 + p * 128:512 + (p + 1) * 128]
        vp = z_ref[0, o, pl.ds(k0, kw), 1024 + p * 128:1024 + (p + 1) * 128]
        zero = jnp.zeros_like(qp)
        q_stack = jnp.concatenate([jnp.where(first, qp, zero), jnp.where(first, zero, qp)], axis=0)
        s = _nt_dot(q_stack, kp)
        es, lses, linvs = [], [], []
        for hh in range(2):
            x = s[hh * 128:(hh + 1) * 128] + bias_ref[bsel, 2 * p + hh, :, 0:kw]
            m = jnp.max(x, axis=1, keepdims=True)
            e = jnp.exp(x - m)
            l = jnp.sum(e, axis=1, keepdims=True)
            es.append(e.astype(BF16))
            lses.append(m + jnp.log(l))
            linvs.append(1.0 / l)
        r = _nn_dot(jnp.concatenate(es, axis=0), vp)
        out = jnp.where(first, r[0:128] * linvs[0], r[128:256] * linvs[1])
        lse = jnp.where(first, lses[0], lses[1])
        o_ref[0, o, pl.ds(q0, 128), p * 128:(p + 1) * 128] = out.astype(o_ref.dtype)
        lse_ref[0, o, pl.ds(q0, 128), p * 128:(p + 1) * 128] = lse

    if nb == 1:
        for o in range(groups):
            for p in range(C_HEADS // 2):
                tile(0, o, p)
    else:
        step = 2
        for o in range(groups):
            def body(j, carry, o=o):
                for dn in range(step):
                    for p in range(C_HEADS // 2):
                        tile(j * step + dn, o, p)
                return carry
            lax.fori_loop(0, nb // step, body, 0)


def _attn_c(zc, dilation, groups):
    b, _, ls, _ = zc.shape
    bias = _c_bias_tables(dilation)
    spec = lambda w: pl.BlockSpec((1, groups, ls, w), lambda bi, j: (bi, j, 0, 0))
    return pl.pallas_call(
        functools.partial(_attn_c_kernel, ls=ls, groups=groups),
        grid=(b, dilation // groups),
        in_specs=[spec(C_COLS), pl.BlockSpec((2, C_HEADS, 128, 256), lambda bi, j: (0, 0, 0, 0))],
        out_specs=[spec(512), spec(512)],
        out_shape=[jax.ShapeDtypeStruct((b, dilation, ls, 512), F32),
                   jax.ShapeDtypeStruct((b, dilation, ls, 512), F32)],
        compiler_params=_params(("parallel", "arbitrary")),
        name=f"attn_c_d{dilation}",
    )(zc, bias)


def _layer_norm(y, g, b):
    mu = jnp.mean(y, axis=-1, keepdims=True)
    d = y - mu
    var = jnp.mean(d * d, axis=-1, keepdims=True)
    return d * lax.rsqrt(var + LN_EPS) * g + b


def _mix_kernel(x_ref, oa_ref, ob_ref, c1_ref, c4_ref, c16_ref, l1_ref, l4_ref, l16_ref, w_ref, g_ref, b_ref,
                y_ref, c4n_ref, c16n_ref, l4n_ref, l16n_ref, *, tm):
    for src, dst, r in ((c4_ref, c4n_ref, 4), (l4_ref, l4n_ref, 4), (c16_ref, c16n_ref, 16), (l16_ref, l16n_ref, 16)):
        for o in range(r):
            for c in range(4):
                dst[c, pl.ds(o, tm // r, stride=r), :] = src[0, o, :, c * 128:(c + 1) * 128]
    natural = lambda ref: jnp.concatenate([ref[c] for c in range(4)], axis=1)
    l0, l1, l2 = l1_ref[...], natural(l4n_ref), natural(l16n_ref)
    m = jnp.maximum(jnp.maximum(l0, l1), l2)
    w0, w1, w2 = jnp.exp(l0 - m), jnp.exp(l1 - m), jnp.exp(l2 - m)
    oc = (w0 * c1_ref[...] + w1 * natural(c4n_ref) + w2 * natural(c16n_ref)) / (w0 + w1 + w2)
    mix = (_nn_dot(oa_ref[...], w_ref[0:256, :]) + _nn_dot(ob_ref[...], w_ref[256:512, :])
           + _nn_dot(oc.astype(BF16), w_ref[512:1024, :]))
    y_ref[...] = _layer_norm(ALPHA * x_ref[...] + mix, g_ref[...], b_ref[...])


def _mix(x2, oa, ob, cs, ls, w_o, g, b, s, tm=512):
    n = x2.shape[0]
    tiles = s // tm
    row = lambda w: pl.BlockSpec((tm, w), lambda i: (i, 0))
    cls = lambda r: pl.BlockSpec((1, r, tm // r, 512), lambda i: (i // tiles, 0, i % tiles, 0))
    const = lambda r, c: pl.BlockSpec((r, c), lambda i: (0, 0))
    return pl.pallas_call(
        functools.partial(_mix_kernel, tm=tm),
        grid=(n // tm,),
        in_specs=[row(D_MODEL), row(256), row(256), row(512), cls(4), cls(16), row(512), cls(4), cls(16),
                  const(D_MODEL, D_MODEL), const(1, D_MODEL), const(1, D_MODEL)],
        out_specs=row(D_MODEL),
        out_shape=jax.ShapeDtypeStruct((n, D_MODEL), F32),
        scratch_shapes=[pltpu.VMEM((4, tm, 128), F32)] * 4,
        compiler_params=_params(("parallel",)),
        name="mix_wo_ln",
    )(x2, oa, ob, cs[0].reshape(n, 512), cs[1], cs[2], ls[0].reshape(n, 512), ls[1], ls[2], w_o, g, b)


FF_CHUNKS = ((0, 768), (768, 768), (1536, 768), (2304, 512))


def _ffn_kernel(x_ref, wg_ref, wu_ref, wd_ref, g_ref, b_ref, y_ref):
    x = x_ref[...]
    xb = x.astype(BF16)
    acc = ALPHA * x
    for start, width in FF_CHUNKS:
        gate = _nn_dot(xb, wg_ref[:, start:start + width])
        up = _nn_dot(xb, wu_ref[:, start:start + width])
        h = (gate * jax.nn.sigmoid(gate) * up).astype(BF16)
        acc = acc + _nn_dot(h, wd_ref[start:start + width, :])
    y_ref[...] = _layer_norm(acc, g_ref[...], b_ref[...])


def _ffn(x2, wg, wu, wd, g, b, tm=512):
    n = x2.shape[0]
    single = pl.Buffered(1)
    return pl.pallas_call(
        _ffn_kernel,
        grid=(n // tm,),
        in_specs=[pl.BlockSpec((tm, D_MODEL), lambda i: (i, 0)),
                  pl.BlockSpec((D_MODEL, FF_DIM), lambda i: (0, 0), pipeline_mode=single),
                  pl.BlockSpec((D_MODEL, FF_DIM), lambda i: (0, 0), pipeline_mode=single),
                  pl.BlockSpec((FF_DIM, D_MODEL), lambda i: (0, 0), pipeline_mode=single),
                  pl.BlockSpec((1, D_MODEL), lambda i: (0, 0)),
                  pl.BlockSpec((1, D_MODEL), lambda i: (0, 0))],
        out_specs=pl.BlockSpec((tm, D_MODEL), lambda i: (i, 0)),
        out_shape=jax.ShapeDtypeStruct((n, D_MODEL), F32),
        compiler_params=_params(("parallel",)),
        name="ffn_ln",
    )(x2, wg, wu, wd, g, b)


C_GROUPS = {1: 1, 4: 4, 16: 4}


def kernel(x, w_in, w_o, lam, subln_g, ln1_g, ln1_b, w_gate, w_up, w_down, ln2_g, ln2_b):
    b, s, d = x.shape
    x2 = x.reshape(b * s, d)
    for l in range(DEPTH):
        lambda_init = 0.8 - 0.6 * math.exp(-0.3 * l)
        za, zb, zc1, zc4, zc16 = _inproj(x2, _pad_w_in(w_in[l]), b, s)
        oa = _attn_a(za.reshape(b, s, A_COLS)).reshape(b * s, 256)
        g_pair = jnp.concatenate([subln_g[l], subln_g[l]]).reshape(1, 128).astype(F32)
        ob = _attn_b(zb.reshape(b, s, B_COLS), lam[l].astype(F32), g_pair, lambda_init).reshape(b * s, 256)
        zcs = (zc1.reshape(b, 1, s, C_COLS), zc4, zc16)
        cs, ls = zip(*[_attn_c(z, dil, C_GROUPS[dil]) for z, dil in zip(zcs, C_DILATIONS)])
        x2 = _mix(x2, oa, ob, cs, ls, w_o[l].astype(BF16),
                  ln1_g[l].reshape(1, d), ln1_b[l].reshape(1, d), s)
        x2 = _ffn(x2, w_gate[l].astype(BF16), w_up[l].astype(BF16), w_down[l].astype(BF16),
                  ln2_g[l].reshape(1, d), ln2_b[l].reshape(1, d))
    return x2.reshape(b, s, d)
```

```python
import functools
import math

import jax
import jax.numpy as jnp
from jax import lax
from jax.experimental import pallas as pl
from jax.experimental.pallas import tpu as pltpu

F32 = jnp.float32
BF16 = jnp.bfloat16
I32 = jnp.int32

D_MODEL = 1024
DEPTH = 2
HEAD_DIM = 64
A_HEADS = 4
IDX_HEADS = 4
TOPK = 256
B_HEADS = 4
DIFF_DH = 32
C_HEADS = 8
C_DILATIONS = (1, 4, 16)
C_STEPS = 128
QB = 128
LANES = 128
N_CLASSES = 4
FF_DIM = 2816
ALPHA = (2 * DEPTH) ** 0.25
LN_EPS = 1e-5
NEG = -1e30

A_COLS = 1024
B_COLS = 768
C_COLS = 1536

VMEM_LIMIT = 56 * 1024 * 1024

INT_MIN = -2 ** 31
KEY_NEG_INF = (-8388608) ^ 0x7FFFFFFF


def _slopes(n):
    return [2.0 ** (-8.0 * (i + 1) / n) for i in range(n)]


def _nt_dot(a, b):
    return lax.dot_general(a, b, (((1,), (1,)), ((), ())), preferred_element_type=F32)


def _nn_dot(a, b):
    return jnp.dot(a, b, preferred_element_type=F32)


def _params(sem):
    return pltpu.CompilerParams(dimension_semantics=sem, vmem_limit_bytes=VMEM_LIMIT)


def _position_columns(rows, width):
    r = lax.broadcasted_iota(I32, (rows, width), 0)
    lane = lax.broadcasted_iota(I32, (rows, width), 1)
    return jnp.where(lane == 0, r & 127, jnp.where(lane == 1, r >> 7, 0)).astype(F32)


def _slope_columns(slope, rows, width):
    lane = lax.broadcasted_iota(I32, (rows, width), 1)
    return jnp.where(lane == 0, slope, jnp.where(lane == 1, slope * 128.0, 0.0)).astype(F32)


def _slab(k):
    return slice(k * LANES, (k + 1) * LANES)


def _row_sum(x):
    return jnp.sum(x, axis=1, keepdims=True)


def _softmax_map(s_ref, e_ref, linv_ref, row0, sk, mask_fn):
    rows = slice(row0, row0 + QB)

    def logits(k):
        x = s_ref[rows, _slab(k)]
        mask = mask_fn(k)
        return x if mask is None else x + mask

    nslab = sk // LANES
    mrun = logits(0)
    for k in range(1, nslab):
        mrun = jnp.maximum(mrun, logits(k))
    m = jnp.max(mrun, axis=1, keepdims=True)
    lrun = jnp.zeros((QB, LANES), F32)
    for k in range(nslab):
        e = jnp.exp(logits(k) - m)
        lrun = lrun + e
        e_ref[rows, _slab(k)] = e.astype(BF16)
    linv_ref[rows, :] = jnp.broadcast_to(1.0 / _row_sum(lrun), (QB, LANES))


def _causal_slab(k, t0):
    row_t = t0 + lax.broadcasted_iota(I32, (QB, LANES), 0)
    col = k * LANES + lax.broadcasted_iota(I32, (QB, LANES), 1)
    return col <= row_t


def _inproj_kernel(x_ref, w32_ref, za_ref, zb_ref, zc1_ref, zc4_ref, zc16_ref, acc_ref, w_ref, *, tm):
    @pl.when(pl.program_id(0) == 0)
    def _():
        for c in range(w32_ref.shape[0] // 128):
            w_ref[:, c * 128:(c + 1) * 128] = w32_ref[c * 128:(c + 1) * 128, :].T.astype(BF16)

    x = x_ref[...].astype(BF16)
    za_ref[...] = _nn_dot(x, w_ref[:, 0:A_COLS])
    zb = _nn_dot(x, w_ref[:, A_COLS:A_COLS + B_COLS])
    lane = lax.broadcasted_iota(I32, (1, B_COLS), 1)
    zb_ref[...] = (zb * jnp.where(lane < 256, DIFF_DH ** -0.5, 1.0)).astype(BF16)
    zc = _nn_dot(x, w_ref[:, A_COLS + B_COLS:])
    lane = lax.broadcasted_iota(I32, (1, C_COLS), 1)
    zc = zc * jnp.where(lane < 512, HEAD_DIM ** -0.5, 1.0)
    zc1_ref[...] = zc.astype(BF16)
    for c in range(C_COLS // 128):
        acc_ref[c] = zc[:, c * 128:(c + 1) * 128]
    for dst, r in ((zc4_ref, 4), (zc16_ref, 16)):
        for o in range(r):
            for c in range(C_COLS // 128):
                dst[0, o, :, c * 128:(c + 1) * 128] = acc_ref[c, pl.ds(o, tm // r, stride=r), :].astype(BF16)


def _inproj(x2, w_pad, b, s, tm=512):
    n = x2.shape[0]
    ncols = A_COLS + B_COLS + C_COLS
    tiles = s // tm
    row = lambda w: pl.BlockSpec((tm, w), lambda i: (i, 0))
    cls = lambda r: pl.BlockSpec((1, r, tm // r, C_COLS), lambda i: (i // tiles, 0, i % tiles, 0))
    return pl.pallas_call(
        functools.partial(_inproj_kernel, tm=tm),
        grid=(n // tm,),
        in_specs=[row(D_MODEL), pl.BlockSpec((ncols, D_MODEL), lambda i: (0, 0), pipeline_mode=pl.Buffered(1))],
        out_specs=[row(A_COLS), row(B_COLS), row(C_COLS), cls(4), cls(16)],
        out_shape=[jax.ShapeDtypeStruct((n, A_COLS), F32),
                   jax.ShapeDtypeStruct((n, B_COLS), BF16),
                   jax.ShapeDtypeStruct((n, C_COLS), BF16),
                   jax.ShapeDtypeStruct((b, 4, s // 4, C_COLS), BF16),
                   jax.ShapeDtypeStruct((b, 16, s // 16, C_COLS), BF16)],
        scratch_shapes=[pltpu.VMEM((C_COLS // 128, tm, 128), F32), pltpu.VMEM((D_MODEL, ncols), BF16)],
        compiler_params=_params(("arbitrary",)),
        name="inproj",
    )(x2, w_pad)


def _pad_w_in(wt):
    def z(k):
        return jnp.zeros((k, D_MODEL), wt.dtype)
    qa, ka, va = wt[0:256], wt[256:320], wt[320:384]
    qi, ki, wi = wt[384:640], wt[640:704], wt[704:708]
    rest = wt[708:]
    return jnp.concatenate([qa, qi, ka, z(64), va, z(64), ki, z(64), wi, z(124), rest], axis=0)


def _attn_a_class(zq_ref, o_ref, kax_ref, vab_ref, kib_ref, s_ref, key_ref, mb_ref, e_ref, linv_ref, thr_ref,
                  wb_ref, *, sk, cls, per):
    nslab = sk // LANES
    first_masked = nslab - per
    kf = float(TOPK)

    def indexer_keys(jj, carry):
        r0 = pl.multiple_of(jj * QB, QB)
        t0 = (cls * per + jj) * QB
        zq = zq_ref[0, pl.ds(r0, QB), :]
        qi = zq[:, 256:512] * 0.125
        zeros64 = jnp.zeros((QB, 64), F32)
        qi_stack = jnp.concatenate(
            [jnp.concatenate([qi[:, h * 64:(h + 1) * 64], zeros64], axis=1) for h in range(IDX_HEADS)], axis=0)
        si = s_ref.at[0]
        si[:, 0:sk] = _nt_dot(qi_stack.astype(BF16), kib_ref[0:sk, :])
        wi = zq[:, 896:900] * 0.5
        for h in range(IDX_HEADS):
            wb_ref[h] = jnp.broadcast_to(wi[:, h:h + 1], (QB, LANES))
        for k in range(nslab):
            score = jnp.maximum(si[0:QB, _slab(k)], 0.0) * wb_ref[0]
            for h in range(1, IDX_HEADS):
                score = score + jnp.maximum(si[h * QB:(h + 1) * QB, _slab(k)], 0.0) * wb_ref[h]
            if k >= first_masked:
                score = jnp.where(_causal_slab(k, t0), score, -jnp.inf)
            bits = lax.bitcast_convert_type(score, I32)
            key_ref[pl.ds(r0, QB), _slab(k)] = jnp.where(bits < 0, bits ^ 0x7FFFFFFF, bits)
        return carry

    lax.fori_loop(0, per, indexer_keys, 0)

    def count_ge(rb, cand):
        rows = slice(rb * QB, (rb + 1) * QB)
        run = jnp.where(key_ref[rows, _slab(0)] >= cand, 1.0, 0.0)
        for k in range(1, nslab):
            run = run + jnp.where(key_ref[rows, _slab(k)] >= cand, 1.0, 0.0)
        return _row_sum(run)

    for rb in range(per):
        zero = jnp.zeros((QB, LANES), I32)
        thr_ref[rb * QB:(rb + 1) * QB, :] = jnp.where(count_ge(rb, zero) >= kf, zero, INT_MIN)

    def search(it, carry):
        bit = lax.shift_left(jnp.int32(1), 30 - it)
        for rb in range(per):
            rows = slice(rb * QB, (rb + 1) * QB)
            thr = thr_ref[rows, :]
            cand = thr | bit
            thr_ref[rows, :] = jnp.where(count_ge(rb, cand) >= kf, cand, thr)
        return carry

    lax.fori_loop(0, 31, search, 0)

    def select_mask(jj, buf):
        r0 = pl.multiple_of(jj * QB, QB)
        t0 = (cls * per + jj) * QB
        mb = mb_ref.at[buf]
        thr = thr_ref[pl.ds(r0, QB), :]
        run = jnp.zeros((QB, LANES), F32)
        for k in range(nslab):
            ge = key_ref[pl.ds(r0, QB), _slab(k)] >= thr
            allowed = jnp.where(_causal_slab(k, t0), 0.0, NEG) if k >= first_masked else 0.0
            mb[:, _slab(k)] = jnp.where(ge, allowed, NEG)
            run = run + jnp.where(ge, 1.0, 0.0)
        thr1 = thr[:, 0:1]
        ambiguous = jnp.where((_row_sum(run) > kf) & (thr1 != KEY_NEG_INF), 1.0, 0.0)

        @pl.when(jnp.max(ambiguous) > 0.0)
        def _():
            r_i = lax.broadcasted_iota(I32, (LANES, LANES), 0)
            c_i = lax.broadcasted_iota(I32, (LANES, LANES), 1)
            upper = jnp.where(r_i < c_i, 1.0, 0.0).astype(BF16)
            n_gt = jnp.zeros((QB, LANES), F32)
            for k in range(nslab):
                n_gt = n_gt + jnp.where(key_ref[pl.ds(r0, QB), _slab(k)] > thr, 1.0, 0.0)
            need = kf - _row_sum(n_gt)
            base = jnp.zeros((QB, 1), F32)
            for k in range(nslab):
                kc = key_ref[pl.ds(r0, QB), _slab(k)]
                eq = kc == thr
                eqf = jnp.where(eq, 1.0, 0.0)
                rank = base + _nn_dot(eqf.astype(BF16), upper)
                sel = (kc > thr) | (eq & (rank < need))
                if k >= first_masked:
                    sel = sel & _causal_slab(k, t0)
                mb[:, _slab(k)] = jnp.where(sel, 0.0, NEG)
                base = base + _row_sum(eqf)

    def attend(jj, buf):
        r0 = pl.multiple_of(jj * QB, QB)
        s_buf, e_buf, linv_buf, mb = s_ref.at[buf], e_ref.at[buf], linv_ref.at[buf], mb_ref.at[buf]
        qa = zq_ref[0, pl.ds(r0, QB), 0:256] * 0.125
        qa_stack = jnp.concatenate(
            [jnp.concatenate([qa[:, h * 64:(h + 1) * 64], _slope_columns(slope, QB, 64)], axis=1)
             for h, slope in enumerate(_slopes(A_HEADS))], axis=0)
        s_buf[:, 0:sk] = _nt_dot(qa_stack.astype(BF16), kax_ref[0:sk, :])
        outs = []
        for h in range(A_HEADS):
            _softmax_map(s_buf, e_buf, linv_buf, h * QB, sk, lambda k: mb[:, _slab(k)])
            rows = slice(h * QB, (h + 1) * QB)
            o = _nn_dot(e_buf[rows, 0:sk], vab_ref[0:sk, :]) * linv_buf[rows, :]
            outs.append(o[:, 0:64])
        o_ref[0, pl.ds(r0, QB), :] = jnp.concatenate(outs, axis=1).astype(o_ref.dtype)

    def attend_pair(it, carry):
        for u in range(2):
            select_mask(2 * it + u, u)
        for u in range(2):
            attend(2 * it + u, u)
        return carry

    lax.fori_loop(0, per // 2, attend_pair, 0)


def _attn_a_kernel(zq_ref, ka_ref, va_ref, ki_ref, o_ref,
                   kax_ref, vab_ref, kib_ref, s_ref, key_ref, mb_ref, e_ref, linv_ref, thr_ref, wb_ref, *, seq):
    c = pl.program_id(1)

    @pl.when(c == 0)
    def _():
        kax_ref[...] = (ka_ref[0] + jnp.concatenate(
            [jnp.zeros((seq, 64), F32), _position_columns(seq, 64)], axis=1)).astype(BF16)
        vab_ref[...] = va_ref[0].astype(BF16)
        kib_ref[...] = ki_ref[0].astype(BF16)

    per = seq // QB // N_CLASSES
    for cls in range(N_CLASSES):
        @pl.when(c == cls)
        def _(cls=cls):
            _attn_a_class(zq_ref, o_ref, kax_ref, vab_ref, kib_ref, s_ref, key_ref, mb_ref, e_ref, linv_ref,
                          thr_ref, wb_ref, sk=(cls + 1) * per * QB, cls=cls, per=per)


def _attn_a(za3):
    b, s, _ = za3.shape
    rows = s // N_CLASSES
    kv_spec = lambda cb: pl.BlockSpec((1, s, 128), lambda bi, c, cb=cb: (bi, 0, cb))
    return pl.pallas_call(
        functools.partial(_attn_a_kernel, seq=s),
        grid=(b, N_CLASSES),
        in_specs=[pl.BlockSpec((1, rows, A_COLS), lambda bi, c: (bi, c, 0)),
                  kv_spec(4), kv_spec(5), kv_spec(6)],
        out_specs=pl.BlockSpec((1, rows, 256), lambda bi, c: (bi, c, 0)),
        out_shape=jax.ShapeDtypeStruct((b, s, 256), BF16),
        scratch_shapes=[pltpu.VMEM((s, 128), BF16), pltpu.VMEM((s, 128), BF16), pltpu.VMEM((s, 128), BF16),
                        pltpu.VMEM((2, 4 * QB, s), F32), pltpu.VMEM((rows, s), I32), pltpu.VMEM((2, QB, s), F32),
                        pltpu.VMEM((2, 4 * QB, s), BF16), pltpu.VMEM((2, 4 * QB, LANES), F32),
                        pltpu.VMEM((rows, LANES), I32), pltpu.VMEM((IDX_HEADS, QB, LANES), F32)],
        compiler_params=_params(("parallel", "arbitrary")),
        name="attn_a",
    )(za3, za3, za3, za3)


def _attn_b_tile(i, q_ref, v_ref, lam_ref, g_ref, o_ref, kx_ref, s_ref, e_ref, linv_ref, *, sk, per, lambda_init):
    nslab = sk // LANES
    first_masked = nslab - per
    t_rel = (i % per) * QB
    masks = [jnp.where(_causal_slab(k, t_rel), 0.0, NEG) for k in range(per)]
    mask_fn = lambda k: masks[k - first_masked] if k >= first_masked else None
    lam = lam_ref[...]
    lam_full = (jnp.exp(jnp.sum(lam[0:1] * lam[1:2], axis=1, keepdims=True))
                - jnp.exp(jnp.sum(lam[2:3] * lam[3:4], axis=1, keepdims=True)) + lambda_init)
    lane = lax.broadcasted_iota(I32, (QB, 128), 1)
    slopes = _slopes(B_HEADS)
    for p in range(B_HEADS // 2):
        qp = q_ref[0, :, p * 128:(p + 1) * 128]
        vp = v_ref[0, 0:sk, p * 128:(p + 1) * 128]
        q_stack = jnp.concatenate(
            [jnp.concatenate([jnp.where((lane >= g * 32) & (lane < (g + 1) * 32), qp, jnp.zeros_like(qp)),
                              _slope_columns(slopes[2 * p + g // 2], QB, 128).astype(BF16)], axis=1)
             for g in range(4)], axis=0)
        s_buf, e_buf, linv_buf = s_ref.at[p], e_ref.at[p], linv_ref.at[p]
        s_buf[:, 0:sk] = _nt_dot(q_stack, kx_ref[p, 0:sk, :])
        r = []
        for g in range(4):
            _softmax_map(s_buf, e_buf, linv_buf, g * QB, sk, mask_fn)
            rows = slice(g * QB, (g + 1) * QB)
            r.append(_nn_dot(e_buf[rows, 0:sk], vp) * linv_buf[rows, :])
        o0 = r[0] - lam_full * r[1]
        o1 = r[2] - lam_full * r[3]
        first = lane < 64
        o = jnp.where(first, o0, o1)
        sq = o * o
        ms0 = _row_sum(jnp.where(first, sq, 0.0))
        ms1 = _row_sum(jnp.where(first, 0.0, sq))
        ms = jnp.where(first, ms0, ms1) * (1.0 / HEAD_DIM)
        o = o * lax.rsqrt(ms + LN_EPS) * g_ref[...] * (1.0 - lambda_init)
        o_ref[0, :, p * 128:(p + 1) * 128] = o.astype(o_ref.dtype)


def _attn_b_kernel(q_ref, k_ref, v_ref, lam_ref, g_ref, o_ref, kx_ref, s_ref, e_ref, linv_ref, *, seq, lambda_init):
    i = pl.program_id(1)

    @pl.when(i == 0)
    def _():
        pos = _position_columns(seq, 128).astype(BF16)
        for p in range(B_HEADS // 2):
            kx_ref[p, :, 0:128] = k_ref[0, :, p * 128:(p + 1) * 128]
            kx_ref[p, :, 128:256] = pos

    per = seq // QB // N_CLASSES
    for c in range(N_CLASSES):
        @pl.when(i // per == c)
        def _(c=c):
            _attn_b_tile(i, q_ref, v_ref, lam_ref, g_ref, o_ref, kx_ref, s_ref, e_ref, linv_ref,
                         sk=(c + 1) * per * QB, per=per, lambda_init=lambda_init)


def _attn_b(zb3, lam, g_pair, lambda_init):
    b, s, _ = zb3.shape
    return pl.pallas_call(
        functools.partial(_attn_b_kernel, seq=s, lambda_init=lambda_init),
        grid=(b, s // QB),
        in_specs=[pl.BlockSpec((1, QB, 256), lambda bi, i: (bi, i, 0)),
                  pl.BlockSpec((1, s, 256), lambda bi, i: (bi, 0, 1)),
                  pl.BlockSpec((1, s, 256), lambda bi, i: (bi, 0, 2)),
                  pl.BlockSpec((4, DIFF_DH), lambda bi, i: (0, 0)),
                  pl.BlockSpec((1, 128), lambda bi, i: (0, 0))],
        out_specs=pl.BlockSpec((1, QB, 256), lambda bi, i: (bi, i, 0)),
        out_shape=jax.ShapeDtypeStruct((b, s, 256), BF16),
        scratch_shapes=[pltpu.VMEM((B_HEADS // 2, s, 256), BF16), pltpu.VMEM((B_HEADS // 2, 4 * QB, s), F32),
                        pltpu.VMEM((B_HEADS // 2, 4 * QB, s), BF16), pltpu.VMEM((B_HEADS // 2, 4 * QB, 128), F32)],
        compiler_params=_params(("parallel", "arbitrary")),
        name="attn_b",
    )(zb3, zb3, zb3, lam, g_pair)


def _c_bias_tables(dilation):
    qi = jnp.arange(128)[:, None]
    kj = jnp.arange(256)[None, :]
    slopes = jnp.asarray(_slopes(C_HEADS), F32)[:, None, None]
    d_first = qi - kj
    d_mid = qi + 128 - kj
    def tab(delta):
        valid = (delta >= 0) & (delta <= C_STEPS)
        return jnp.where(valid[None], -slopes * (delta * dilation).astype(F32)[None], NEG)
    return jnp.stack([tab(d_first), tab(d_mid)]).astype(F32)


def _attn_c_kernel(z_ref, bias_ref, o_ref, lse_ref, *, ls, groups):
    nb = ls // 128
    lane = lax.broadcasted_iota(I32, (128, 128), 1)
    first = lane < 64

    def tile(n, o, p):
        if nb == 1:
            q0, k0, kw, bsel = 0, 0, 128, 0
        else:
            q0 = pl.multiple_of(n * 128, 128)
            k0 = pl.multiple_of(jnp.maximum(n - 1, 0) * 128, 128)
            kw, bsel = 256, jnp.minimum(n, 1)
        qp = z_ref[0, o, pl.ds(q0, 128), p * 128:(p + 1) * 128]
        kp = z_ref[0, o, pl.ds(k0, kw), 512 + p * 128:512 + (p + 1) * 128]
        vp = z_ref[0, o, pl.ds(k0, kw), 1024 + p * 128:1024 + (p + 1) * 128]
        zero = jnp.zeros_like(qp)
        q_stack = jnp.concatenate([jnp.where(first, qp, zero), jnp.where(first, zero, qp)], axis=0)
        s = _nt_dot(q_stack, kp)
        es, lses, linvs = [], [], []
        for hh in range(2):
            x = s[hh * 128:(hh + 1) * 128] + bias_ref[bsel, 2 * p + hh, :, 0:kw]
            m = jnp.max(x, axis=1, keepdims=True)
            e = jnp.exp(x - m)
            l = jnp.sum(e, axis=1, keepdims=True)
            es.append(e.astype(BF16))
            lses.append(m + jnp.log(l))
            linvs.append(1.0 / l)
        r = _nn_dot(jnp.concatenate(es, axis=0), vp)
        out = jnp.where(first, r[0:128] * linvs[0], r[128:256] * linvs[1])
        lse = jnp.where(first, lses[0], lses[1])
        o_ref[0, o, pl.ds(q0, 128), p * 128:(p + 1) * 128] = out.astype(o_ref.dtype)
        lse_ref[0, o, pl.ds(q0, 128), p * 128:(p + 1) * 128] = lse

    if nb == 1:
        for o in range(groups):
            for p in range(C_HEADS // 2):
                tile(0, o, p)
    else:
        step = 2
        for o in range(groups):
            def body(j, carry, o=o):
                for dn in range(step):
                    for p in range(C_HEADS // 2):
                        tile(j * step + dn, o, p)
                return carry
            lax.fori_loop(0, nb // step, body, 0)


def _attn_c(zc, dilation, groups):
    b, _, ls, _ = zc.shape
    bias = _c_bias_tables(dilation)
    spec = lambda w: pl.BlockSpec((1, groups, ls, w), lambda bi, j: (bi, j, 0, 0))
    return pl.pallas_call(
        functools.partial(_attn_c_kernel, ls=ls, groups=groups),
        grid=(b, dilation // groups),
        in_specs=[spec(C_COLS), pl.BlockSpec((2, C_HEADS, 128, 256), lambda bi, j: (0, 0, 0, 0))],
        out_specs=[spec(512), spec(512)],
        out_shape=[jax.ShapeDtypeStruct((b, dilation, ls, 512), F32),
                   jax.ShapeDtypeStruct((b, dilation, ls, 512), F32)],
        compiler_params=_params(("parallel", "arbitrary")),
        name=f"attn_c_d{dilation}",
    )(zc, bias)


def _layer_norm(y, g, b):
    mu = jnp.mean(y, axis=-1, keepdims=True)
    d = y - mu
    var = jnp.mean(d * d, axis=-1, keepdims=True)
    return d * lax.rsqrt(var + LN_EPS) * g + b


def _mix_kernel(x_ref, oa_ref, ob_ref, c1_ref, c4_ref, c16_ref, l1_ref, l4_ref, l16_ref, w_ref, g_ref, b_ref,
                y_ref, c4n_ref, c16n_ref, l4n_ref, l16n_ref, *, tm):
    for src, dst, r in ((c4_ref, c4n_ref, 4), (l4_ref, l4n_ref, 4), (c16_ref, c16n_ref, 16), (l16_ref, l16n_ref, 16)):
        for o in range(r):
            for c in range(4):
                dst[c, pl.ds(o, tm // r, stride=r), :] = src[0, o, :, c * 128:(c + 1) * 128]
    natural = lambda ref: jnp.concatenate([ref[c] for c in range(4)], axis=1)
    l0, l1, l2 = l1_ref[...], natural(l4n_ref), natural(l16n_ref)
    m = jnp.maximum(jnp.maximum(l0, l1), l2)
    w0, w1, w2 = jnp.exp(l0 - m), jnp.exp(l1 - m), jnp.exp(l2 - m)
    oc = (w0 * c1_ref[...] + w1 * natural(c4n_ref) + w2 * natural(c16n_ref)) / (w0 + w1 + w2)
    mix = (_nn_dot(oa_ref[...], w_ref[0:256, :]) + _nn_dot(ob_ref[...], w_ref[256:512, :])
           + _nn_dot(oc.astype(BF16), w_ref[512:1024, :]))
    y_ref[...] = _layer_norm(ALPHA * x_ref[...] + mix, g_ref[...], b_ref[...])


def _mix(x2, oa, ob, cs, ls, w_o, g, b, s, tm=512):
    n = x2.shape[0]
    tiles = s // tm
    row = lambda w: pl.BlockSpec((tm, w), lambda i: (i, 0))
    cls = lambda r: pl.BlockSpec((1, r, tm // r, 512), lambda i: (i // tiles, 0, i % tiles, 0))
    const = lambda r, c: pl.BlockSpec((r, c), lambda i: (0, 0))
    return pl.pallas_call(
        functools.partial(_mix_kernel, tm=tm),
        grid=(n // tm,),
        in_specs=[row(D_MODEL), row(256), row(256), row(512), cls(4), cls(16), row(512), cls(4), cls(16),
                  const(D_MODEL, D_MODEL), const(1, D_MODEL), const(1, D_MODEL)],
        out_specs=row(D_MODEL),
        out_shape=jax.ShapeDtypeStruct((n, D_MODEL), F32),
        scratch_shapes=[pltpu.VMEM((4, tm, 128), F32)] * 4,
        compiler_params=_params(("parallel",)),
        name="mix_wo_ln",
    )(x2, oa, ob, cs[0].reshape(n, 512), cs[1], cs[2], ls[0].reshape(n, 512), ls[1], ls[2], w_o, g, b)


FF_CHUNKS = ((0, 768), (768, 768), (1536, 768), (2304, 512))


def _ffn_kernel(x_ref, wg_ref, wu_ref, wd_ref, g_ref, b_ref, y_ref):
    x = x_ref[...]
    xb = x.astype(BF16)
    acc = ALPHA * x
    for start, width in FF_CHUNKS:
        gate = _nn_dot(xb, wg_ref[:, start:start + width])
        up = _nn_dot(xb, wu_ref[:, start:start + width])
        h = (gate * jax.nn.sigmoid(gate) * up).astype(BF16)
        acc = acc + _nn_dot(h, wd_ref[start:start + width, :])
    y_ref[...] = _layer_norm(acc, g_ref[...], b_ref[...])


def _ffn(x2, wg, wu, wd, g, b, tm=512):
    n = x2.shape[0]
    single = pl.Buffered(1)
    return pl.pallas_call(
        _ffn_kernel,
        grid=(n // tm,),
        in_specs=[pl.BlockSpec((tm, D_MODEL), lambda i: (i, 0)),
                  pl.BlockSpec((D_MODEL, FF_DIM), lambda i: (0, 0), pipeline_mode=single),
                  pl.BlockSpec((D_MODEL, FF_DIM), lambda i: (0, 0), pipeline_mode=single),
                  pl.BlockSpec((FF_DIM, D_MODEL), lambda i: (0, 0), pipeline_mode=single),
                  pl.BlockSpec((1, D_MODEL), lambda i: (0, 0)),
                  pl.BlockSpec((1, D_MODEL), lambda i: (0, 0))],
        out_specs=pl.BlockSpec((tm, D_MODEL), lambda i: (i, 0)),
        out_shape=jax.ShapeDtypeStruct((n, D_MODEL), F32),
        compiler_params=_params(("parallel",)),
        name="ffn_ln",
    )(x2, wg, wu, wd, g, b)


C_GROUPS = {1: 1, 4: 4, 16: 4}


def kernel(x, w_in, w_o, lam, subln_g, ln1_g, ln1_b, w_gate, w_up, w_down, ln2_g, ln2_b):
    b, s, d = x.shape
    x2 = x.reshape(b * s, d)
    w_in_t = jnp.transpose(w_in, (2, 0, 1))
    for l in range(DEPTH):
        lambda_init = 0.8 - 0.6 * math.exp(-0.3 * l)
        za, zb, zc1, zc4, zc16 = _inproj(x2, _pad_w_in(w_in_t[:, l, :]), b, s)
        oa = _attn_a(za.reshape(b, s, A_COLS)).reshape(b * s, 256)
        g_pair = jnp.concatenate([subln_g[l], subln_g[l]]).reshape(1, 128).astype(F32)
        ob = _attn_b(zb.reshape(b, s, B_COLS), lam[l].astype(F32), g_pair, lambda_init).reshape(b * s, 256)
        zcs = (zc1.reshape(b, 1, s, C_COLS), zc4, zc16)
        cs, ls = zip(*[_attn_c(z, dil, C_GROUPS[dil]) for z, dil in zip(zcs, C_DILATIONS)])
        x2 = _mix(x2, oa, ob, cs, ls, w_o[l].astype(BF16),
                  ln1_g[l].reshape(1, d), ln1_b[l].reshape(1, d), s)
        x2 = _ffn(x2, w_gate[l].astype(BF16), w_up[l].astype(BF16), w_down[l].astype(BF16),
                  ln2_g[l].reshape(1, d), ln2_b[l].reshape(1, d))
    return x2.reshape(b, s, d)
```

```python
import functools
import math

import jax
import jax.numpy as jnp
from jax import lax
from jax.experimental import pallas as pl
from jax.experimental.pallas import tpu as pltpu

F32 = jnp.float32
BF16 = jnp.bfloat16
I32 = jnp.int32
I16 = jnp.int16

D_MODEL = 1024
DEPTH = 2
HEAD_DIM = 64
A_HEADS = 4
IDX_HEADS = 4
TOPK = 256
B_HEADS = 4
DIFF_DH = 32
C_HEADS = 8
C_DILATIONS = (1, 4, 16)
C_STEPS = 128
QB = 128
LANES = 128
N_CLASSES = 4
FF_DIM = 2816
ALPHA = (2 * DEPTH) ** 0.25
LN_EPS = 1e-5
NEG = -1e30

A_COLS = 1024
B_COLS = 768
C_COLS = 1536

VMEM_LIMIT = 56 * 1024 * 1024

INT_MIN = -2 ** 31
KEY_NEG_INF = (-8388608) ^ 0x7FFFFFFF


def _slopes(n):
    return [2.0 ** (-8.0 * (i + 1) / n) for i in range(n)]


def _nt_dot(a, b):
    return lax.dot_general(a, b, (((1,), (1,)), ((), ())), preferred_element_type=F32)


def _nn_dot(a, b):
    return jnp.dot(a, b, preferred_element_type=F32)


def _params(sem):
    return pltpu.CompilerParams(dimension_semantics=sem, vmem_limit_bytes=VMEM_LIMIT)


def _position_columns(rows, width):
    r = lax.broadcasted_iota(I32, (rows, width), 0)
    lane = lax.broadcasted_iota(I32, (rows, width), 1)
    return jnp.where(lane == 0, r & 127, jnp.where(lane == 1, r >> 7, 0)).astype(F32)


def _slope_columns(slope, rows, width):
    lane = lax.broadcasted_iota(I32, (rows, width), 1)
    return jnp.where(lane == 0, slope, jnp.where(lane == 1, slope * 128.0, 0.0)).astype(F32)


def _slab(k):
    return slice(k * LANES, (k + 1) * LANES)


def _row_sum(x):
    return jnp.sum(x, axis=1, keepdims=True)


def _softmax_map(s_ref, e_ref, linv_ref, row0, sk, mask_fn):
    rows = slice(row0, row0 + QB)

    def logits(k):
        x = s_ref[rows, _slab(k)]
        mask = mask_fn(k)
        return x if mask is None else x + mask

    nslab = sk // LANES
    mrun = logits(0)
    for k in range(1, nslab):
        mrun = jnp.maximum(mrun, logits(k))
    m = jnp.max(mrun, axis=1, keepdims=True)
    lrun = jnp.zeros((QB, LANES), F32)
    for k in range(nslab):
        e = jnp.exp(logits(k) - m)
        lrun = lrun + e
        e_ref[rows, _slab(k)] = e.astype(BF16)
    linv_ref[rows, :] = jnp.broadcast_to(1.0 / _row_sum(lrun), (QB, LANES))


def _causal_slab(k, t0):
    row_t = t0 + lax.broadcasted_iota(I32, (QB, LANES), 0)
    col = k * LANES + lax.broadcasted_iota(I32, (QB, LANES), 1)
    return col <= row_t


def _inproj_kernel(x_ref, w32_ref, za_ref, zb_ref, zc1_ref, zc4_ref, zc16_ref, acc_ref, w_ref, *, tm):
    @pl.when(pl.program_id(0) == 0)
    def _():
        for c in range(w32_ref.shape[0] // 128):
            w_ref[:, c * 128:(c + 1) * 128] = w32_ref[c * 128:(c + 1) * 128, :].T.astype(BF16)

    x = x_ref[...].astype(BF16)
    za_ref[...] = _nn_dot(x, w_ref[:, 0:A_COLS])
    zb = _nn_dot(x, w_ref[:, A_COLS:A_COLS + B_COLS])
    lane = lax.broadcasted_iota(I32, (1, B_COLS), 1)
    zb_ref[...] = (zb * jnp.where(lane < 256, DIFF_DH ** -0.5, 1.0)).astype(BF16)
    zc = _nn_dot(x, w_ref[:, A_COLS + B_COLS:])
    lane = lax.broadcasted_iota(I32, (1, C_COLS), 1)
    zc = zc * jnp.where(lane < 512, HEAD_DIM ** -0.5, 1.0)
    zc1_ref[...] = zc.astype(BF16)
    for c in range(C_COLS // 128):
        acc_ref[c] = zc[:, c * 128:(c + 1) * 128]
    for dst, r in ((zc4_ref, 4), (zc16_ref, 16)):
        for o in range(r):
            for c in range(C_COLS // 128):
                dst[0, o, :, c * 128:(c + 1) * 128] = acc_ref[c, pl.ds(o, tm // r, stride=r), :].astype(BF16)


def _inproj(x2, w_pad, b, s, tm=512):
    n = x2.shape[0]
    ncols = A_COLS + B_COLS + C_COLS
    tiles = s // tm
    row = lambda w: pl.BlockSpec((tm, w), lambda i: (i, 0))
    cls = lambda r: pl.BlockSpec((1, r, tm // r, C_COLS), lambda i: (i // tiles, 0, i % tiles, 0))
    return pl.pallas_call(
        functools.partial(_inproj_kernel, tm=tm),
        grid=(n // tm,),
        in_specs=[row(D_MODEL), pl.BlockSpec((ncols, D_MODEL), lambda i: (0, 0), pipeline_mode=pl.Buffered(1))],
        out_specs=[row(A_COLS), row(B_COLS), row(C_COLS), cls(4), cls(16)],
        out_shape=[jax.ShapeDtypeStruct((n, A_COLS), F32),
                   jax.ShapeDtypeStruct((n, B_COLS), BF16),
                   jax.ShapeDtypeStruct((n, C_COLS), BF16),
                   jax.ShapeDtypeStruct((b, 4, s // 4, C_COLS), BF16),
                   jax.ShapeDtypeStruct((b, 16, s // 16, C_COLS), BF16)],
        scratch_shapes=[pltpu.VMEM((C_COLS // 128, tm, 128), F32), pltpu.VMEM((D_MODEL, ncols), BF16)],
        compiler_params=_params(("arbitrary",)),
        name="inproj",
    )(x2, w_pad)


def _pad_w_in(wt):
    def z(k):
        return jnp.zeros((k, D_MODEL), wt.dtype)
    qa, ka, va = wt[0:256], wt[256:320], wt[320:384]
    qi, ki, wi = wt[384:640], wt[640:704], wt[704:708]
    rest = wt[708:]
    return jnp.concatenate([qa, qi, ka, z(64), va, z(64), ki, z(64), wi, z(124), rest], axis=0)


def _attn_a_class(zq_ref, o_ref, kax_ref, vab_ref, kib_ref, s_ref, key_ref, mb_ref, e_ref, linv_ref, thr_ref,
                  wb_ref, hi_ref, lo_ref, thi_ref, need_ref, *, sk, cls, per):
    nslab = sk // LANES
    first_masked = nslab - per
    kf = float(TOPK)

    def indexer_keys(jj, carry):
        r0 = pl.multiple_of(jj * QB, QB)
        t0 = (cls * per + jj) * QB
        zq = zq_ref[0, pl.ds(r0, QB), :]
        qi = zq[:, 256:512] * 0.125
        zeros64 = jnp.zeros((QB, 64), F32)
        qi_stack = jnp.concatenate(
            [jnp.concatenate([qi[:, h * 64:(h + 1) * 64], zeros64], axis=1) for h in range(IDX_HEADS)], axis=0)
        si = s_ref.at[0]
        si[:, 0:sk] = _nt_dot(qi_stack.astype(BF16), kib_ref[0:sk, :])
        wi = zq[:, 896:900] * 0.5
        for h in range(IDX_HEADS):
            wb_ref[h] = jnp.broadcast_to(wi[:, h:h + 1], (QB, LANES))
        for k in range(nslab):
            score = jnp.maximum(si[0:QB, _slab(k)], 0.0) * wb_ref[0]
            for h in range(1, IDX_HEADS):
                score = score + jnp.maximum(si[h * QB:(h + 1) * QB, _slab(k)], 0.0) * wb_ref[h]
            if k >= first_masked:
                score = jnp.where(_causal_slab(k, t0), score, -jnp.inf)
            bits = lax.bitcast_convert_type(score, I32)
            key = jnp.where(bits < 0, bits ^ 0x7FFFFFFF, bits)
            key_ref[pl.ds(r0, QB), _slab(k)] = key
            hi_ref[pl.ds(r0, QB), _slab(k)] = (key >> 16).astype(I16)
            lo_ref[pl.ds(r0, QB), _slab(k)] = ((key & 0xFFFF) - 32768).astype(I16)
        return carry

    lax.fori_loop(0, per, indexer_keys, 0)

    first_rb = 2 if cls == 0 else 0
    for rb in range(first_rb):
        thr_ref[rb * QB:(rb + 1) * QB, :] = jnp.full((QB, LANES), KEY_NEG_INF, I32)
    active = list(range(first_rb, per))
    rb_slabs = {rb: cls * per + rb + 1 for rb in active}
    one16, zero16 = jnp.ones((QB, LANES), I16), jnp.zeros((QB, LANES), I16)

    def count16(ref, rb, pred):
        rows = slice(rb * QB, (rb + 1) * QB)
        run = jnp.where(pred(ref[rows, _slab(0)]), one16, zero16)
        for k in range(1, rb_slabs[rb]):
            run = run + jnp.where(pred(ref[rows, _slab(k)]), one16, zero16)
        return _row_sum(run.astype(F32))

    def half_search(ref, t_ref):
        for rb in active:
            rows = slice(rb * QB, (rb + 1) * QB)
            nonneg = count16(ref, rb, lambda x: x >= zero16) >= need_ref[rows, :]
            t_ref[rows, :] = jnp.where(nonneg, 0, -32768).astype(I32)

        def step(it, carry):
            bit = lax.shift_left(jnp.int32(1), 14 - it)
            for rb in active:
                rows = slice(rb * QB, (rb + 1) * QB)
                t = t_ref[rows, :]
                cand = t | bit
                cand16 = cand.astype(I16)
                ok = count16(ref, rb, lambda x: x >= cand16) >= need_ref[rows, :]
                t_ref[rows, :] = jnp.where(ok, cand, t)
            return carry

        lax.fori_loop(0, 15, step, 0)

    for rb in active:
        need_ref[rb * QB:(rb + 1) * QB, :] = jnp.full((QB, LANES), kf, F32)
    half_search(hi_ref, thi_ref)
    for rb in active:
        rows = slice(rb * QB, (rb + 1) * QB)
        thi16 = thi_ref[rows, :].astype(I16)
        need_ref[rows, :] = jnp.broadcast_to(kf - count16(hi_ref, rb, lambda x: x > thi16), (QB, LANES))
        for k in range(rb_slabs[rb]):
            lo_ref[rows, _slab(k)] = jnp.where(hi_ref[rows, _slab(k)] == thi16, lo_ref[rows, _slab(k)],
                                               jnp.full((QB, LANES), -32768, I16))
    half_search(lo_ref, thr_ref)
    for rb in active:
        rows = slice(rb * QB, (rb + 1) * QB)
        thr_ref[rows, :] = lax.shift_left(thi_ref[rows, :], 16) | ((thr_ref[rows, :] + 32768) & 0xFFFF)

    def select_mask(jj, buf):
        r0 = pl.multiple_of(jj * QB, QB)
        t0 = (cls * per + jj) * QB
        mb = mb_ref.at[buf]
        thr = thr_ref[pl.ds(r0, QB), :]
        run = jnp.zeros((QB, LANES), F32)
        for k in range(nslab):
            ge = key_ref[pl.ds(r0, QB), _slab(k)] >= thr
            allowed = jnp.where(_causal_slab(k, t0), 0.0, NEG) if k >= first_masked else 0.0
            mb[:, _slab(k)] = jnp.where(ge, allowed, NEG)
            run = run + jnp.where(ge, 1.0, 0.0)
        thr1 = thr[:, 0:1]
        ambiguous = jnp.where((_row_sum(run) > kf) & (thr1 != KEY_NEG_INF), 1.0, 0.0)

        @pl.when(jnp.max(ambiguous) > 0.0)
        def _():
            r_i = lax.broadcasted_iota(I32, (LANES, LANES), 0)
            c_i = lax.broadcasted_iota(I32, (LANES, LANES), 1)
            upper = jnp.where(r_i < c_i, 1.0, 0.0).astype(BF16)
            n_gt = jnp.zeros((QB, LANES), F32)
            for k in range(nslab):
                n_gt = n_gt + jnp.where(key_ref[pl.ds(r0, QB), _slab(k)] > thr, 1.0, 0.0)
            need = kf - _row_sum(n_gt)
            base = jnp.zeros((QB, 1), F32)
            for k in range(nslab):
                kc = key_ref[pl.ds(r0, QB), _slab(k)]
                eq = kc == thr
                eqf = jnp.where(eq, 1.0, 0.0)
                rank = base + _nn_dot(eqf.astype(BF16), upper)
                sel = (kc > thr) | (eq & (rank < need))
                if k >= first_masked:
                    sel = sel & _causal_slab(k, t0)
                mb[:, _slab(k)] = jnp.where(sel, 0.0, NEG)
                base = base + _row_sum(eqf)

    def attend(jj, buf):
        r0 = pl.multiple_of(jj * QB, QB)
        s_buf, e_buf, linv_buf, mb = s_ref.at[buf], e_ref.at[buf], linv_ref.at[buf], mb_ref.at[buf]
        qa = zq_ref[0, pl.ds(r0, QB), 0:256] * 0.125
        qa_stack = jnp.concatenate(
            [jnp.concatenate([qa[:, h * 64:(h + 1) * 64], _slope_columns(slope, QB, 64)], axis=1)
             for h, slope in enumerate(_slopes(A_HEADS))], axis=0)
        s_buf[:, 0:sk] = _nt_dot(qa_stack.astype(BF16), kax_ref[0:sk, :])
        outs = []
        for h in range(A_HEADS):
            _softmax_map(s_buf, e_buf, linv_buf, h * QB, sk, lambda k: mb[:, _slab(k)])
            rows = slice(h * QB, (h + 1) * QB)
            o = _nn_dot(e_buf[rows, 0:sk], vab_ref[0:sk, :]) * linv_buf[rows, :]
            outs.append(o[:, 0:64])
        o_ref[0, pl.ds(r0, QB), :] = jnp.concatenate(outs, axis=1).astype(o_ref.dtype)

    def attend_pair(it, carry):
        for u in range(2):
            select_mask(2 * it + u, u)
        for u in range(2):
            attend(2 * it + u, u)
        return carry

    lax.fori_loop(0, per // 2, attend_pair, 0)


def _attn_a_kernel(zq_ref, ka_ref, va_ref, ki_ref, o_ref,
                   kax_ref, vab_ref, kib_ref, s_ref, key_ref, mb_ref, e_ref, linv_ref, thr_ref, wb_ref,
                   hi_ref, lo_ref, thi_ref, need_ref, *, seq):
    c = pl.program_id(1)

    @pl.when(c == 0)
    def _():
        kax_ref[...] = (ka_ref[0] + jnp.concatenate(
            [jnp.zeros((seq, 64), F32), _position_columns(seq, 64)], axis=1)).astype(BF16)
        vab_ref[...] = va_ref[0].astype(BF16)
        kib_ref[...] = ki_ref[0].astype(BF16)

    per = seq // QB // N_CLASSES
    for cls in range(N_CLASSES):
        @pl.when(c == cls)
        def _(cls=cls):
            _attn_a_class(zq_ref, o_ref, kax_ref, vab_ref, kib_ref, s_ref, key_ref, mb_ref, e_ref, linv_ref,
                          thr_ref, wb_ref, hi_ref, lo_ref, thi_ref, need_ref,
                          sk=(cls + 1) * per * QB, cls=cls, per=per)


def _attn_a(za3):
    b, s, _ = za3.shape
    rows = s // N_CLASSES
    kv_spec = lambda cb: pl.BlockSpec((1, s, 128), lambda bi, c, cb=cb: (bi, 0, cb))
    return pl.pallas_call(
        functools.partial(_attn_a_kernel, seq=s),
        grid=(b, N_CLASSES),
        in_specs=[pl.BlockSpec((1, rows, A_COLS), lambda bi, c: (bi, c, 0)),
                  kv_spec(4), kv_spec(5), kv_spec(6)],
        out_specs=pl.BlockSpec((1, rows, 256), lambda bi, c: (bi, c, 0)),
        out_shape=jax.ShapeDtypeStruct((b, s, 256), BF16),
        scratch_shapes=[pltpu.VMEM((s, 128), BF16), pltpu.VMEM((s, 128), BF16), pltpu.VMEM((s, 128), BF16),
                        pltpu.VMEM((2, 4 * QB, s), F32), pltpu.VMEM((rows, s), I32), pltpu.VMEM((2, QB, s), F32),
                        pltpu.VMEM((2, 4 * QB, s), BF16), pltpu.VMEM((2, 4 * QB, LANES), F32),
                        pltpu.VMEM((rows, LANES), I32), pltpu.VMEM((IDX_HEADS, QB, LANES), F32),
                        pltpu.VMEM((rows, s), I16), pltpu.VMEM((rows, s), I16),
                        pltpu.VMEM((rows, LANES), I32), pltpu.VMEM((rows, LANES), F32)],
        compiler_params=_params(("parallel", "arbitrary")),
        name="attn_a",
    )(za3, za3, za3, za3)


def _attn_b_tile(i, q_ref, v_ref, lam_ref, g_ref, o_ref, kx_ref, s_ref, e_ref, linv_ref, *, sk, per, lambda_init):
    nslab = sk // LANES
    first_masked = nslab - per
    t_rel = (i % per) * QB
    masks = [jnp.where(_causal_slab(k, t_rel), 0.0, NEG) for k in range(per)]
    mask_fn = lambda k: masks[k - first_masked] if k >= first_masked else None
    lam = lam_ref[...]
    lam_full = (jnp.exp(jnp.sum(lam[0:1] * lam[1:2], axis=1, keepdims=True))
                - jnp.exp(jnp.sum(lam[2:3] * lam[3:4], axis=1, keepdims=True)) + lambda_init)
    lane = lax.broadcasted_iota(I32, (QB, 128), 1)
    slopes = _slopes(B_HEADS)
    for p in range(B_HEADS // 2):
        qp = q_ref[0, :, p * 128:(p + 1) * 128]
        vp = v_ref[0, 0:sk, p * 128:(p + 1) * 128]
        q_stack = jnp.concatenate(
            [jnp.concatenate([jnp.where((lane >= g * 32) & (lane < (g + 1) * 32), qp, jnp.zeros_like(qp)),
                              _slope_columns(slopes[2 * p + g // 2], QB, 128).astype(BF16)], axis=1)
             for g in range(4)], axis=0)
        s_buf, e_buf, linv_buf = s_ref.at[p], e_ref.at[p], linv_ref.at[p]
        s_buf[:, 0:sk] = _nt_dot(q_stack, kx_ref[p, 0:sk, :])
        r = []
        for g in range(4):
            _softmax_map(s_buf, e_buf, linv_buf, g * QB, sk, mask_fn)
            rows = slice(g * QB, (g + 1) * QB)
            r.append(_nn_dot(e_buf[rows, 0:sk], vp) * linv_buf[rows, :])
        o0 = r[0] - lam_full * r[1]
        o1 = r[2] - lam_full * r[3]
        first = lane < 64
        o = jnp.where(first, o0, o1)
        sq = o * o
        ms0 = _row_sum(jnp.where(first, sq, 0.0))
        ms1 = _row_sum(jnp.where(first, 0.0, sq))
        ms = jnp.where(first, ms0, ms1) * (1.0 / HEAD_DIM)
        o = o * lax.rsqrt(ms + LN_EPS) * g_ref[...] * (1.0 - lambda_init)
        o_ref[0, :, p * 128:(p + 1) * 128] = o.astype(o_ref.dtype)


def _attn_b_kernel(q_ref, k_ref, v_ref, lam_ref, g_ref, o_ref, kx_ref, s_ref, e_ref, linv_ref, *, seq, lambda_init):
    i = pl.program_id(1)

    @pl.when(i == 0)
    def _():
        pos = _position_columns(seq, 128).astype(BF16)
        for p in range(B_HEADS // 2):
            kx_ref[p, :, 0:128] = k_ref[0, :, p * 128:(p + 1) * 128]
            kx_ref[p, :, 128:256] = pos

    per = seq // QB // N_CLASSES
    for c in range(N_CLASSES):
        @pl.when(i // per == c)
        def _(c=c):
            _attn_b_tile(i, q_ref, v_ref, lam_ref, g_ref, o_ref, kx_ref, s_ref, e_ref, linv_ref,
                         sk=(c + 1) * per * QB, per=per, lambda_init=lambda_init)


def _attn_b(zb3, lam, g_pair, lambda_init):
    b, s, _ = zb3.shape
    return pl.pallas_call(
        functools.partial(_attn_b_kernel, seq=s, lambda_init=lambda_init),
        grid=(b, s // QB),
        in_specs=[pl.BlockSpec((1, QB, 256), lambda bi, i: (bi, i, 0)),
                  pl.BlockSpec((1, s, 256), lambda bi, i: (bi, 0, 1)),
                  pl.BlockSpec((1, s, 256), lambda bi, i: (bi, 0, 2)),
                  pl.BlockSpec((4, DIFF_DH), lambda bi, i: (0, 0)),
                  pl.BlockSpec((1, 128), lambda bi, i: (0, 0))],
        out_specs=pl.BlockSpec((1, QB, 256), lambda bi, i: (bi, i, 0)),
        out_shape=jax.ShapeDtypeStruct((b, s, 256), BF16),
        scratch_shapes=[pltpu.VMEM((B_HEADS // 2, s, 256), BF16), pltpu.VMEM((B_HEADS // 2, 4 * QB, s), F32),
                        pltpu.VMEM((B_HEADS // 2, 4 * QB, s), BF16), pltpu.VMEM((B_HEADS // 2, 4 * QB, 128), F32)],
        compiler_params=_params(("parallel", "arbitrary")),
        name="attn_b",
    )(zb3, zb3, zb3, lam, g_pair)


def _c_bias_tables(dilation):
    qi = jnp.arange(128)[:, None]
    kj = jnp.arange(256)[None, :]
    slopes = jnp.asarray(_slopes(C_HEADS), F32)[:, None, None]
    d_first = qi - kj
    d_mid = qi + 128 - kj
    def tab(delta):
        valid = (delta >= 0) & (delta <= C_STEPS)
        return jnp.where(valid[None], -slopes * (delta * dilation).astype(F32)[None], NEG)
    return jnp.stack([tab(d_first), tab(d_mid)]).astype(F32)


def _attn_c_kernel(z_ref, bias_ref, o_ref, lse_ref, *, ls, groups):
    nb = ls // 128
    lane = lax.broadcasted_iota(I32, (128, 128), 1)
    first = lane < 64

    def tile(n, o, p):
        if nb == 1:
            q0, k0, kw, bsel = 0, 0, 128, 0
        else:
            q0 = pl.multiple_of(n * 128, 128)
            k0 = pl.multiple_of(jnp.maximum(n - 1, 0) * 128, 128)
            kw, bsel = 256, jnp.minimum(n, 1)
        qp = z_ref[0, o, pl.ds(q0, 128), p * 128:(p + 1) * 128]
        kp = z_ref[0, o, pl.ds(k0, kw), 512 + p * 128:512 + (p + 1) * 128]
        vp = z_ref[0, o, pl.ds(k0, kw), 1024 + p * 128:1024 + (p + 1) * 128]
        zero = jnp.zeros_like(qp)
        q_stack = jnp.concatenate([jnp.where(first, qp, zero), jnp.where(first, zero, qp)], axis=0)
        s = _nt_dot(q_stack, kp)
        es, lses, linvs = [], [], []
        for hh in range(2):
            x = s[hh * 128:(hh + 1) * 128] + bias_ref[bsel, 2 * p + hh, :, 0:kw]
            m = jnp.max(x, axis=1, keepdims=True)
            e = jnp.exp(x - m)
            l = jnp.sum(e, axis=1, keepdims=True)
            es.append(e.astype(BF16))
            lses.append(m + jnp.log(l))
            linvs.append(1.0 / l)
        r = _nn_dot(jnp.concatenate(es, axis=0), vp)
        out = jnp.where(first, r[0:128] * linvs[0], r[128:256] * linvs[1])
        lse = jnp.where(first, lses[0], lses[1])
        o_ref[0, o, pl.ds(q0, 128), p * 128:(p + 1) * 128] = out.astype(o_ref.dtype)
        lse_ref[0, o, pl.ds(q0, 128), p * 128:(p + 1) * 128] = lse

    if nb == 1:
        for o in range(groups):
            for p in range(C_HEADS // 2):
                tile(0, o, p)
    else:
        step = 4
        for o in range(groups):
            def body(j, carry, o=o):
                for dn in range(step):
                    for p in range(C_HEADS // 2):
                        tile(j * step + dn, o, p)
                return carry
            lax.fori_loop(0, nb // step, body, 0)


def _attn_c(zc, dilation, groups):
    b, _, ls, _ = zc.shape
    bias = _c_bias_tables(dilation)
    spec = lambda w: pl.BlockSpec((1, groups, ls, w), lambda bi, j: (bi, j, 0, 0))
    return pl.pallas_call(
        functools.partial(_attn_c_kernel, ls=ls, groups=groups),
        grid=(b, dilation // groups),
        in_specs=[spec(C_COLS), pl.BlockSpec((2, C_HEADS, 128, 256), lambda bi, j: (0, 0, 0, 0))],
        out_specs=[spec(512), spec(512)],
        out_shape=[jax.ShapeDtypeStruct((b, dilation, ls, 512), BF16),
                   jax.ShapeDtypeStruct((b, dilation, ls, 512), F32)],
        compiler_params=_params(("parallel", "arbitrary")),
        name=f"attn_c_d{dilation}",
    )(zc, bias)


def _layer_norm(y, g, b):
    mu = jnp.mean(y, axis=-1, keepdims=True)
    d = y - mu
    var = jnp.mean(d * d, axis=-1, keepdims=True)
    return d * lax.rsqrt(var + LN_EPS) * g + b


def _mix_kernel(x_ref, oa_ref, ob_ref, c1_ref, c4_ref, c16_ref, l1_ref, l4_ref, l16_ref, w_ref, g_ref, b_ref,
                y_ref, c4n_ref, c16n_ref, l4n_ref, l16n_ref, *, tm):
    for src, dst, r in ((c4_ref, c4n_ref, 4), (l4_ref, l4n_ref, 4), (c16_ref, c16n_ref, 16), (l16_ref, l16n_ref, 16)):
        for o in range(r):
            for c in range(4):
                dst[c, pl.ds(o, tm // r, stride=r), :] = src[0, o, :, c * 128:(c + 1) * 128].astype(F32)
    natural = lambda ref: jnp.concatenate([ref[c] for c in range(4)], axis=1)
    l0, l1, l2 = l1_ref[...], natural(l4n_ref), natural(l16n_ref)
    m = jnp.maximum(jnp.maximum(l0, l1), l2)
    w0, w1, w2 = jnp.exp(l0 - m), jnp.exp(l1 - m), jnp.exp(l2 - m)
    oc = (w0 * c1_ref[...].astype(F32) + w1 * natural(c4n_ref) + w2 * natural(c16n_ref)) / (w0 + w1 + w2)
    mix = (_nn_dot(oa_ref[...], w_ref[0:256, :]) + _nn_dot(ob_ref[...], w_ref[256:512, :])
           + _nn_dot(oc.astype(BF16), w_ref[512:1024, :]))
    y_ref[...] = _layer_norm(ALPHA * x_ref[...] + mix, g_ref[...], b_ref[...])


def _mix(x2, oa, ob, cs, ls, w_o, g, b, s, tm=512):
    n = x2.shape[0]
    tiles = s // tm
    row = lambda w: pl.BlockSpec((tm, w), lambda i: (i, 0))
    cls = lambda r: pl.BlockSpec((1, r, tm // r, 512), lambda i: (i // tiles, 0, i % tiles, 0))
    const = lambda r, c: pl.BlockSpec((r, c), lambda i: (0, 0))
    return pl.pallas_call(
        functools.partial(_mix_kernel, tm=tm),
        grid=(n // tm,),
        in_specs=[row(D_MODEL), row(256), row(256), row(512), cls(4), cls(16), row(512), cls(4), cls(16),
                  const(D_MODEL, D_MODEL), const(1, D_MODEL), const(1, D_MODEL)],
        out_specs=row(D_MODEL),
        out_shape=jax.ShapeDtypeStruct((n, D_MODEL), F32),
        scratch_shapes=[pltpu.VMEM((4, tm, 128), F32)] * 4,
        compiler_params=_params(("parallel",)),
        name="mix_wo_ln",
    )(x2, oa, ob, cs[0].reshape(n, 512), cs[1], cs[2], ls[0].reshape(n, 512), ls[1], ls[2], w_o, g, b)


FF_CHUNKS = ((0, 768), (768, 768), (1536, 768), (2304, 512))


def _ffn_kernel(x_ref, wg_ref, wu_ref, wd_ref, g_ref, b_ref, y_ref):
    x = x_ref[...]
    xb = x.astype(BF16)
    acc = ALPHA * x
    for start, width in FF_CHUNKS:
        gate = _nn_dot(xb, wg_ref[:, start:start + width])
        up = _nn_dot(xb, wu_ref[:, start:start + width])
        h = (gate * jax.nn.sigmoid(gate) * up).astype(BF16)
        acc = acc + _nn_dot(h, wd_ref[start:start + width, :])
    y_ref[...] = _layer_norm(acc, g_ref[...], b_ref[...])


def _ffn(x2, wg, wu, wd, g, b, tm=512):
    n = x2.shape[0]
    single = pl.Buffered(1)
    return pl.pallas_call(
        _ffn_kernel,
        grid=(n // tm,),
        in_specs=[pl.BlockSpec((tm, D_MODEL), lambda i: (i, 0)),
                  pl.BlockSpec((D_MODEL, FF_DIM), lambda i: (0, 0), pipeline_mode=single),
                  pl.BlockSpec((D_MODEL, FF_DIM), lambda i: (0, 0), pipeline_mode=single),
                  pl.BlockSpec((FF_DIM, D_MODEL), lambda i: (0, 0), pipeline_mode=single),
                  pl.BlockSpec((1, D_MODEL), lambda i: (0, 0)),
                  pl.BlockSpec((1, D_MODEL), lambda i: (0, 0))],
        out_specs=pl.BlockSpec((tm, D_MODEL), lambda i: (i, 0)),
        out_shape=jax.ShapeDtypeStruct((n, D_MODEL), F32),
        compiler_params=_params(("parallel",)),
        name="ffn_ln",
    )(x2, wg, wu, wd, g, b)


C_GROUPS = {1: 1, 4: 4, 16: 4}


def kernel(x, w_in, w_o, lam, subln_g, ln1_g, ln1_b, w_gate, w_up, w_down, ln2_g, ln2_b):
    b, s, d = x.shape
    x2 = x.reshape(b * s, d)
    w_in_t = jnp.transpose(w_in, (2, 0, 1))
    for l in range(DEPTH):
        lambda_init = 0.8 - 0.6 * math.exp(-0.3 * l)
        za, zb, zc1, zc4, zc16 = _inproj(x2, _pad_w_in(w_in_t[:, l, :]), b, s)
        oa = _attn_a(za.reshape(b, s, A_COLS)).reshape(b * s, 256)
        g_pair = jnp.concatenate([subln_g[l], subln_g[l]]).reshape(1, 128).astype(F32)
        ob = _attn_b(zb.reshape(b, s, B_COLS), lam[l].astype(F32), g_pair, lambda_init).reshape(b * s, 256)
        zcs = (zc1.reshape(b, 1, s, C_COLS), zc4, zc16)
        cs, ls = zip(*[_attn_c(z, dil, C_GROUPS[dil]) for z, dil in zip(zcs, C_DILATIONS)])
        x2 = _mix(x2, oa, ob, cs, ls, w_o[l].astype(BF16),
                  ln1_g[l].reshape(1, d), ln1_b[l].reshape(1, d), s)
        x2 = _ffn(x2, w_gate[l].astype(BF16), w_up[l].astype(BF16), w_down[l].astype(BF16),
                  ln2_g[l].reshape(1, d), ln2_b[l].reshape(1, d))
    return x2.reshape(b, s, d)
```

```python
import functools
import math

import jax
import jax.numpy as jnp
from jax import lax
from jax.experimental import pallas as pl
from jax.experimental.pallas import tpu as pltpu

F32 = jnp.float32
BF16 = jnp.bfloat16
I32 = jnp.int32

D_MODEL = 1024
DEPTH = 2
HEAD_DIM = 64
A_HEADS = 4
IDX_HEADS = 4
TOPK = 256
B_HEADS = 4
DIFF_DH = 32
C_HEADS = 8
C_DILATIONS = (1, 4, 16)
C_STEPS = 128
QB = 128
LANES = 128
N_CLASSES = 4
N_CLASSES_B = 8
FF_DIM = 2816
ALPHA = (2 * DEPTH) ** 0.25
LN_EPS = 1e-5
NEG = -1e30

A_COLS = 1024
B_COLS = 768
C_COLS = 1536

VMEM_LIMIT = 56 * 1024 * 1024

INT_MIN = -2 ** 31
KEY_NEG_INF = (-8388608) ^ 0x7FFFFFFF


def _slopes(n):
    return [2.0 ** (-8.0 * (i + 1) / n) for i in range(n)]


def _nt_dot(a, b):
    return lax.dot_general(a, b, (((1,), (1,)), ((), ())), preferred_element_type=F32)


def _nn_dot(a, b):
    return jnp.dot(a, b, preferred_element_type=F32)


def _params(sem):
    return pltpu.CompilerParams(dimension_semantics=sem, vmem_limit_bytes=VMEM_LIMIT)


def _position_columns(rows, width):
    r = lax.broadcasted_iota(I32, (rows, width), 0)
    lane = lax.broadcasted_iota(I32, (rows, width), 1)
    return jnp.where(lane == 0, r & 127, jnp.where(lane == 1, r >> 7, 0)).astype(F32)


def _slope_columns(slope, rows, width):
    lane = lax.broadcasted_iota(I32, (rows, width), 1)
    return jnp.where(lane == 0, slope, jnp.where(lane == 1, slope * 128.0, 0.0)).astype(F32)


def _slab(k):
    return slice(k * LANES, (k + 1) * LANES)


def _row_sum(x):
    return jnp.sum(x, axis=1, keepdims=True)


def _softmax_map(s_ref, e_ref, linv_ref, row0, sk, mask_fn):
    rows = slice(row0, row0 + QB)

    def logits(k):
        x = s_ref[rows, _slab(k)]
        mask = mask_fn(k)
        return x if mask is None else x + mask

    nslab = sk // LANES
    mrun = logits(0)
    for k in range(1, nslab):
        mrun = jnp.maximum(mrun, logits(k))
    m = jnp.max(mrun, axis=1, keepdims=True)
    lrun = jnp.zeros((QB, LANES), F32)
    for k in range(nslab):
        e = jnp.exp(logits(k) - m)
        lrun = lrun + e
        e_ref[rows, _slab(k)] = e.astype(BF16)
    linv_ref[rows, :] = jnp.broadcast_to(1.0 / _row_sum(lrun), (QB, LANES))


def _causal_slab(k, t0):
    row_t = t0 + lax.broadcasted_iota(I32, (QB, LANES), 0)
    col = k * LANES + lax.broadcasted_iota(I32, (QB, LANES), 1)
    return col <= row_t


def _inproj_kernel(x_ref, w32_ref, za_ref, zb_ref, zc1_ref, zc4_ref, zc16_ref, acc_ref, w_ref, *, tm):
    @pl.when(pl.program_id(0) == 0)
    def _():
        for c in range(w32_ref.shape[0] // 128):
            w_ref[:, c * 128:(c + 1) * 128] = w32_ref[c * 128:(c + 1) * 128, :].T.astype(BF16)

    x = x_ref[...].astype(BF16)
    za_ref[...] = _nn_dot(x, w_ref[:, 0:A_COLS])
    zb = _nn_dot(x, w_ref[:, A_COLS:A_COLS + B_COLS])
    lane = lax.broadcasted_iota(I32, (1, B_COLS), 1)
    zb_ref[...] = (zb * jnp.where(lane < 256, DIFF_DH ** -0.5, 1.0)).astype(BF16)
    zc = _nn_dot(x, w_ref[:, A_COLS + B_COLS:])
    lane = lax.broadcasted_iota(I32, (1, C_COLS), 1)
    zc = zc * jnp.where(lane < 512, HEAD_DIM ** -0.5, 1.0)
    zc1_ref[...] = zc.astype(BF16)
    for c in range(C_COLS // 128):
        acc_ref[c] = zc[:, c * 128:(c + 1) * 128]
    for dst, r in ((zc4_ref, 4), (zc16_ref, 16)):
        for o in range(r):
            for c in range(C_COLS // 128):
                dst[0, o, :, c * 128:(c + 1) * 128] = acc_ref[c, pl.ds(o, tm // r, stride=r), :].astype(BF16)


def _inproj(x2, w_pad, b, s, tm=512):
    n = x2.shape[0]
    ncols = A_COLS + B_COLS + C_COLS
    tiles = s // tm
    row = lambda w: pl.BlockSpec((tm, w), lambda i: (i, 0))
    cls = lambda r: pl.BlockSpec((1, r, tm // r, C_COLS), lambda i: (i // tiles, 0, i % tiles, 0))
    return pl.pallas_call(
        functools.partial(_inproj_kernel, tm=tm),
        grid=(n // tm,),
        in_specs=[row(D_MODEL), pl.BlockSpec((ncols, D_MODEL), lambda i: (0, 0), pipeline_mode=pl.Buffered(1))],
        out_specs=[row(A_COLS), row(B_COLS), row(C_COLS), cls(4), cls(16)],
        out_shape=[jax.ShapeDtypeStruct((n, A_COLS), F32),
                   jax.ShapeDtypeStruct((n, B_COLS), BF16),
                   jax.ShapeDtypeStruct((n, C_COLS), BF16),
                   jax.ShapeDtypeStruct((b, 4, s // 4, C_COLS), BF16),
                   jax.ShapeDtypeStruct((b, 16, s // 16, C_COLS), BF16)],
        scratch_shapes=[pltpu.VMEM((C_COLS // 128, tm, 128), F32), pltpu.VMEM((D_MODEL, ncols), BF16)],
        compiler_params=_params(("arbitrary",)),
        name="inproj",
    )(x2, w_pad)


def _pad_w_in(wt):
    def z(k):
        return jnp.zeros((k, D_MODEL), wt.dtype)
    qa, ka, va = wt[0:256], wt[256:320], wt[320:384]
    qi, ki, wi = wt[384:640], wt[640:704], wt[704:708]
    rest = wt[708:]
    return jnp.concatenate([qa, qi, ka, z(64), va, z(64), ki, z(64), wi, z(124), rest], axis=0)


def _attn_a_class(zq_ref, o_ref, kax_ref, vab_ref, kib_ref, s_ref, key_ref, mb_ref, e_ref, linv_ref, thr_ref,
                  wb_ref, *, cls, per):
    kf = float(TOPK)
    nslabs = [cls * per + jj + 1 for jj in range(per)]

    def indexer_keys(jj):
        r0, t0, ns = jj * QB, (cls * per + jj) * QB, nslabs[jj]
        ext = ns * LANES
        zq = zq_ref[0, r0:r0 + QB, :]
        qi = zq[:, 256:512] * 0.125
        zeros64 = jnp.zeros((QB, 64), F32)
        qi_stack = jnp.concatenate(
            [jnp.concatenate([qi[:, h * 64:(h + 1) * 64], zeros64], axis=1) for h in range(IDX_HEADS)], axis=0)
        si, wb = s_ref.at[jj % 2], wb_ref.at[jj % 2]
        si[:, 0:ext] = _nt_dot(qi_stack.astype(BF16), kib_ref[0:ext, :])
        wi = zq[:, 896:900] * 0.5
        for h in range(IDX_HEADS):
            wb[h] = jnp.broadcast_to(wi[:, h:h + 1], (QB, LANES))
        for k in range(ns):
            score = jnp.maximum(si[0:QB, _slab(k)], 0.0) * wb[0]
            for h in range(1, IDX_HEADS):
                score = score + jnp.maximum(si[h * QB:(h + 1) * QB, _slab(k)], 0.0) * wb[h]
            if k == ns - 1:
                score = jnp.where(_causal_slab(k, t0), score, -jnp.inf)
            bits = lax.bitcast_convert_type(score, I32)
            key_ref[r0:r0 + QB, _slab(k)] = jnp.where(bits < 0, bits ^ 0x7FFFFFFF, bits)

    for jj in range(per):
        indexer_keys(jj)

    first_rb = 2 if cls == 0 else 0
    for rb in range(first_rb):
        thr_ref[rb * QB:(rb + 1) * QB, :] = jnp.full((QB, LANES), KEY_NEG_INF, I32)
    active = list(range(first_rb, per))

    def count_ge(rb, cand):
        rows = slice(rb * QB, (rb + 1) * QB)
        run = jnp.where(key_ref[rows, _slab(0)] >= cand, 1.0, 0.0)
        for k in range(1, nslabs[rb]):
            run = run + jnp.where(key_ref[rows, _slab(k)] >= cand, 1.0, 0.0)
        return _row_sum(run)

    for rb in active:
        zero = jnp.zeros((QB, LANES), I32)
        thr_ref[rb * QB:(rb + 1) * QB, :] = jnp.where(count_ge(rb, zero) >= kf, zero, INT_MIN)

    def search(it, carry):
        bit = lax.shift_left(jnp.int32(1), 30 - it)
        for rb in active:
            rows = slice(rb * QB, (rb + 1) * QB)
            thr = thr_ref[rows, :]
            cand = thr | bit
            thr_ref[rows, :] = jnp.where(count_ge(rb, cand) >= kf, cand, thr)
        return carry

    lax.fori_loop(0, 31, search, 0)

    def select_mask(jj, buf):
        r0, t0, ns = jj * QB, (cls * per + jj) * QB, nslabs[jj]
        rows = slice(r0, r0 + QB)
        mb = mb_ref.at[buf]
        thr = thr_ref[rows, :]
        run = jnp.zeros((QB, LANES), F32)
        for k in range(ns):
            ge = key_ref[rows, _slab(k)] >= thr
            allowed = jnp.where(_causal_slab(k, t0), 0.0, NEG) if k == ns - 1 else 0.0
            mb[:, _slab(k)] = jnp.where(ge, allowed, NEG)
            run = run + jnp.where(ge, 1.0, 0.0)
        thr1 = thr[:, 0:1]
        ambiguous = jnp.where((_row_sum(run) > kf) & (thr1 != KEY_NEG_INF), 1.0, 0.0)

        @pl.when(jnp.max(ambiguous) > 0.0)
        def _():
            r_i = lax.broadcasted_iota(I32, (LANES, LANES), 0)
            c_i = lax.broadcasted_iota(I32, (LANES, LANES), 1)
            upper = jnp.where(r_i < c_i, 1.0, 0.0).astype(BF16)
            n_gt = jnp.zeros((QB, LANES), F32)
            for k in range(ns):
                n_gt = n_gt + jnp.where(key_ref[rows, _slab(k)] > thr, 1.0, 0.0)
            need = kf - _row_sum(n_gt)
            base = jnp.zeros((QB, 1), F32)
            for k in range(ns):
                kc = key_ref[rows, _slab(k)]
                eq = kc == thr
                eqf = jnp.where(eq, 1.0, 0.0)
                rank = base + _nn_dot(eqf.astype(BF16), upper)
                sel = (kc > thr) | (eq & (rank < need))
                if k == ns - 1:
                    sel = sel & _causal_slab(k, t0)
                mb[:, _slab(k)] = jnp.where(sel, 0.0, NEG)
                base = base + _row_sum(eqf)

    def attend(jj, buf):
        r0, ext = jj * QB, nslabs[jj] * LANES
        s_buf, e_buf, linv_buf, mb = s_ref.at[buf], e_ref.at[buf], linv_ref.at[buf], mb_ref.at[buf]
        qa = zq_ref[0, r0:r0 + QB, 0:256] * 0.125
        qa_stack = jnp.concatenate(
            [jnp.concatenate([qa[:, h * 64:(h + 1) * 64], _slope_columns(slope, QB, 64)], axis=1)
             for h, slope in enumerate(_slopes(A_HEADS))], axis=0)
        s_buf[:, 0:ext] = _nt_dot(qa_stack.astype(BF16), kax_ref[0:ext, :])
        outs = []
        for h in range(A_HEADS):
            _softmax_map(s_buf, e_buf, linv_buf, h * QB, ext, lambda k: mb[:, _slab(k)])
            rows = slice(h * QB, (h + 1) * QB)
            o = _nn_dot(e_buf[rows, 0:ext], vab_ref[0:ext, :]) * linv_buf[rows, :]
            outs.append(o[:, 0:64])
        o_ref[0, r0:r0 + QB, :] = jnp.concatenate(outs, axis=1).astype(o_ref.dtype)

    for it in range(per // 2):
        for u in range(2):
            select_mask(2 * it + u, u)
        for u in range(2):
            attend(2 * it + u, u)


def _attn_a_kernel(zq_ref, ka_ref, va_ref, ki_ref, o_ref,
                   kax_ref, vab_ref, kib_ref, s_ref, key_ref, mb_ref, e_ref, linv_ref, thr_ref, wb_ref, *, seq):
    c = pl.program_id(1)

    @pl.when(c == 0)
    def _():
        kax_ref[...] = (ka_ref[0] + jnp.concatenate(
            [jnp.zeros((seq, 64), F32), _position_columns(seq, 64)], axis=1)).astype(BF16)
        vab_ref[...] = va_ref[0].astype(BF16)
        kib_ref[...] = ki_ref[0].astype(BF16)

    per = seq // QB // N_CLASSES
    for cls in range(N_CLASSES):
        @pl.when(c == cls)
        def _(cls=cls):
            _attn_a_class(zq_ref, o_ref, kax_ref, vab_ref, kib_ref, s_ref, key_ref, mb_ref, e_ref, linv_ref,
                          thr_ref, wb_ref, cls=cls, per=per)


def _attn_a(za3):
    b, s, _ = za3.shape
    rows = s // N_CLASSES
    kv_spec = lambda cb: pl.BlockSpec((1, s, 128), lambda bi, c, cb=cb: (bi, 0, cb))
    return pl.pallas_call(
        functools.partial(_attn_a_kernel, seq=s),
        grid=(b, N_CLASSES),
        in_specs=[pl.BlockSpec((1, rows, A_COLS), lambda bi, c: (bi, c, 0)),
                  kv_spec(4), kv_spec(5), kv_spec(6)],
        out_specs=pl.BlockSpec((1, rows, 256), lambda bi, c: (bi, c, 0)),
        out_shape=jax.ShapeDtypeStruct((b, s, 256), BF16),
        scratch_shapes=[pltpu.VMEM((s, 128), BF16), pltpu.VMEM((s, 128), BF16), pltpu.VMEM((s, 128), BF16),
                        pltpu.VMEM((2, 4 * QB, s), F32), pltpu.VMEM((rows, s), I32), pltpu.VMEM((2, QB, s), F32),
                        pltpu.VMEM((2, 4 * QB, s), BF16), pltpu.VMEM((2, 4 * QB, LANES), F32),
                        pltpu.VMEM((rows, LANES), I32), pltpu.VMEM((2, IDX_HEADS, QB, LANES), F32)],
        compiler_params=_params(("parallel", "arbitrary")),
        name="attn_a",
    )(za3, za3, za3, za3)


def _attn_b_tile(i, q_ref, v_ref, lam_ref, g_ref, o_ref, kx_ref, s_ref, e_ref, linv_ref, *, sk, per, lambda_init):
    nslab = sk // LANES
    first_masked = nslab - per
    t_rel = (i % per) * QB
    masks = [jnp.where(_causal_slab(k, t_rel), 0.0, NEG) for k in range(per)]
    mask_fn = lambda k: masks[k - first_masked] if k >= first_masked else None
    lam = lam_ref[...]
    lam_full = (jnp.exp(jnp.sum(lam[0:1] * lam[1:2], axis=1, keepdims=True))
                - jnp.exp(jnp.sum(lam[2:3] * lam[3:4], axis=1, keepdims=True)) + lambda_init)
    lane = lax.broadcasted_iota(I32, (QB, 128), 1)
    slopes = _slopes(B_HEADS)
    for p in range(B_HEADS // 2):
        qp = q_ref[0, :, p * 128:(p + 1) * 128]
        vp = v_ref[0, 0:sk, p * 128:(p + 1) * 128]
        q_stack = jnp.concatenate(
            [jnp.concatenate([jnp.where((lane >= g * 32) & (lane < (g + 1) * 32), qp, jnp.zeros_like(qp)),
                              _slope_columns(slopes[2 * p + g // 2], QB, 128).astype(BF16)], axis=1)
             for g in range(4)], axis=0)
        s_buf, e_buf, linv_buf = s_ref.at[p], e_ref.at[p], linv_ref.at[p]
        s_buf[:, 0:sk] = _nt_dot(q_stack, kx_ref[p, 0:sk, :])
        r = []
        for g in range(4):
            _softmax_map(s_buf, e_buf, linv_buf, g * QB, sk, mask_fn)
            rows = slice(g * QB, (g + 1) * QB)
            r.append(_nn_dot(e_buf[rows, 0:sk], vp) * linv_buf[rows, :])
        o0 = r[0] - lam_full * r[1]
        o1 = r[2] - lam_full * r[3]
        first = lane < 64
        o = jnp.where(first, o0, o1)
        sq = o * o
        ms0 = _row_sum(jnp.where(first, sq, 0.0))
        ms1 = _row_sum(jnp.where(first, 0.0, sq))
        ms = jnp.where(first, ms0, ms1) * (1.0 / HEAD_DIM)
        o = o * lax.rsqrt(ms + LN_EPS) * g_ref[...] * (1.0 - lambda_init)
        o_ref[0, :, p * 128:(p + 1) * 128] = o.astype(o_ref.dtype)


def _attn_b_kernel(q_ref, k_ref, v_ref, lam_ref, g_ref, o_ref, kx_ref, s_ref, e_ref, linv_ref, *, seq, lambda_init):
    i = pl.program_id(1)

    @pl.when(i == 0)
    def _():
        pos = _position_columns(seq, 128).astype(BF16)
        for p in range(B_HEADS // 2):
            kx_ref[p, :, 0:128] = k_ref[0, :, p * 128:(p + 1) * 128]
            kx_ref[p, :, 128:256] = pos

    per = seq // QB // N_CLASSES_B
    for c in range(N_CLASSES_B):
        @pl.when(i // per == c)
        def _(c=c):
            _attn_b_tile(i, q_ref, v_ref, lam_ref, g_ref, o_ref, kx_ref, s_ref, e_ref, linv_ref,
                         sk=(c + 1) * per * QB, per=per, lambda_init=lambda_init)


def _attn_b(zb3, lam, g_pair, lambda_init):
    b, s, _ = zb3.shape
    return pl.pallas_call(
        functools.partial(_attn_b_kernel, seq=s, lambda_init=lambda_init),
        grid=(b, s // QB),
        in_specs=[pl.BlockSpec((1, QB, 256), lambda bi, i: (bi, i, 0)),
                  pl.BlockSpec((1, s, 256), lambda bi, i: (bi, 0, 1)),
                  pl.BlockSpec((1, s, 256), lambda bi, i: (bi, 0, 2)),
                  pl.BlockSpec((4, DIFF_DH), lambda bi, i: (0, 0)),
                  pl.BlockSpec((1, 128), lambda bi, i: (0, 0))],
        out_specs=pl.BlockSpec((1, QB, 256), lambda bi, i: (bi, i, 0)),
        out_shape=jax.ShapeDtypeStruct((b, s, 256), BF16),
        scratch_shapes=[pltpu.VMEM((B_HEADS // 2, s, 256), BF16), pltpu.VMEM((B_HEADS // 2, 4 * QB, s), F32),
                        pltpu.VMEM((B_HEADS // 2, 4 * QB, s), BF16), pltpu.VMEM((B_HEADS // 2, 4 * QB, 128), F32)],
        compiler_params=_params(("parallel", "arbitrary")),
        name="attn_b",
    )(zb3, zb3, zb3, lam, g_pair)


def _c_bias_tables(dilation):
    qi = jnp.arange(128)[:, None]
    kj = jnp.arange(256)[None, :]
    slopes = jnp.asarray(_slopes(C_HEADS), F32)[:, None, None]
    d_first = qi - kj
    d_mid = qi + 128 - kj
    def tab(delta):
        valid = (delta >= 0) & (delta <= C_STEPS)
        return jnp.where(valid[None], -slopes * (delta * dilation).astype(F32)[None], NEG)
    return jnp.stack([tab(d_first), tab(d_mid)]).astype(F32)


def _attn_c_kernel(z_ref, bias_ref, o_ref, lse_ref, *, ls, groups):
    nb = ls // 128
    lane = lax.broadcasted_iota(I32, (128, 128), 1)
    first = lane < 64

    def tile(n, o, p):
        if nb == 1:
            q0, k0, kw, bsel = 0, 0, 128, 0
        else:
            q0 = pl.multiple_of(n * 128, 128)
            k0 = pl.multiple_of(jnp.maximum(n - 1, 0) * 128, 128)
            kw, bsel = 256, jnp.minimum(n, 1)
        qp = z_ref[0, o, pl.ds(q0, 128), p * 128:(p + 1) * 128]
        kp = z_ref[0, o, pl.ds(k0, kw), 512 + p * 128:512 + (p + 1) * 128]
        vp = z_ref[0, o, pl.ds(k0, kw), 1024 + p * 128:1024 + (p + 1) * 128]
        zero = jnp.zeros_like(qp)
        q_stack = jnp.concatenate([jnp.where(first, qp, zero), jnp.where(first, zero, qp)], axis=0)
        s = _nt_dot(q_stack, kp)
        es, lses, linvs = [], [], []
        for hh in range(2):
            x = s[hh * 128:(hh + 1) * 128] + bias_ref[bsel, 2 * p + hh, :, 0:kw]
            m = jnp.max(x, axis=1, keepdims=True)
            e = jnp.exp(x - m)
            l = jnp.sum(e, axis=1, keepdims=True)
            es.append(e.astype(BF16))
            lses.append(m + jnp.log(l))
            linvs.append(1.0 / l)
        r = _nn_dot(jnp.concatenate(es, axis=0), vp)
        out = jnp.where(first, r[0:128] * linvs[0], r[128:256] * linvs[1])
        lse = jnp.where(first, lses[0], lses[1])
        o_ref[0, o, pl.ds(q0, 128), p * 128:(p + 1) * 128] = out.astype(o_ref.dtype)
        lse_ref[0, o, pl.ds(q0, 128), p * 128:(p + 1) * 128] = lse

    if nb == 1:
        for o in range(groups):
            for p in range(C_HEADS // 2):
                tile(0, o, p)
    else:
        step = 4
        for o in range(groups):
            def body(j, carry, o=o):
                for dn in range(step):
                    for p in range(C_HEADS // 2):
                        tile(j * step + dn, o, p)
                return carry
            lax.fori_loop(0, nb // step, body, 0)


def _attn_c(zc, dilation, groups):
    b, _, ls, _ = zc.shape
    bias = _c_bias_tables(dilation)
    spec = lambda w: pl.BlockSpec((1, groups, ls, w), lambda bi, j: (bi, j, 0, 0))
    return pl.pallas_call(
        functools.partial(_attn_c_kernel, ls=ls, groups=groups),
        grid=(b, dilation // groups),
        in_specs=[spec(C_COLS), pl.BlockSpec((2, C_HEADS, 128, 256), lambda bi, j: (0, 0, 0, 0))],
        out_specs=[spec(512), spec(512)],
        out_shape=[jax.ShapeDtypeStruct((b, dilation, ls, 512), BF16),
                   jax.ShapeDtypeStruct((b, dilation, ls, 512), F32)],
        compiler_params=_params(("parallel", "arbitrary")),
        name=f"attn_c_d{dilation}",
    )(zc, bias)


def _layer_norm(y, g, b):
    mu = jnp.mean(y, axis=-1, keepdims=True)
    d = y - mu
    var = jnp.mean(d * d, axis=-1, keepdims=True)
    return d * lax.rsqrt(var + LN_EPS) * g + b


def _mix_kernel(x_ref, oa_ref, ob_ref, c1_ref, c4_ref, c16_ref, l1_ref, l4_ref, l16_ref, w_ref, g_ref, b_ref,
                y_ref, c4n_ref, c16n_ref, l4n_ref, l16n_ref, *, tm):
    for src, dst, r in ((c4_ref, c4n_ref, 4), (l4_ref, l4n_ref, 4), (c16_ref, c16n_ref, 16), (l16_ref, l16n_ref, 16)):
        for o in range(r):
            for c in range(4):
                dst[c, pl.ds(o, tm // r, stride=r), :] = src[0, o, :, c * 128:(c + 1) * 128].astype(F32)
    natural = lambda ref: jnp.concatenate([ref[c] for c in range(4)], axis=1)
    l0, l1, l2 = l1_ref[...], natural(l4n_ref), natural(l16n_ref)
    m = jnp.maximum(jnp.maximum(l0, l1), l2)
    w0, w1, w2 = jnp.exp(l0 - m), jnp.exp(l1 - m), jnp.exp(l2 - m)
    oc = (w0 * c1_ref[...].astype(F32) + w1 * natural(c4n_ref) + w2 * natural(c16n_ref)) / (w0 + w1 + w2)
    mix = (_nn_dot(oa_ref[...], w_ref[0:256, :]) + _nn_dot(ob_ref[...], w_ref[256:512, :])
           + _nn_dot(oc.astype(BF16), w_ref[512:1024, :]))
    y_ref[...] = _layer_norm(ALPHA * x_ref[...] + mix, g_ref[...], b_ref[...])


def _mix(x2, oa, ob, cs, ls, w_o, g, b, s, tm=512):
    n = x2.shape[0]
    tiles = s // tm
    row = lambda w: pl.BlockSpec((tm, w), lambda i: (i, 0))
    cls = lambda r: pl.BlockSpec((1, r, tm // r, 512), lambda i: (i // tiles, 0, i % tiles, 0))
    const = lambda r, c: pl.BlockSpec((r, c), lambda i: (0, 0))
    return pl.pallas_call(
        functools.partial(_mix_kernel, tm=tm),
        grid=(n // tm,),
        in_specs=[row(D_MODEL), row(256), row(256), row(512), cls(4), cls(16), row(512), cls(4), cls(16),
                  const(D_MODEL, D_MODEL), const(1, D_MODEL), const(1, D_MODEL)],
        out_specs=row(D_MODEL),
        out_shape=jax.ShapeDtypeStruct((n, D_MODEL), F32),
        scratch_shapes=[pltpu.VMEM((4, tm, 128), F32)] * 4,
        compiler_params=_params(("parallel",)),
        name="mix_wo_ln",
    )(x2, oa, ob, cs[0].reshape(n, 512), cs[1], cs[2], ls[0].reshape(n, 512), ls[1], ls[2], w_o, g, b)


FF_CHUNKS = ((0, 768), (768, 768), (1536, 768), (2304, 512))


def _ffn_kernel(x_ref, wg_ref, wu_ref, wd_ref, g_ref, b_ref, y_ref):
    x = x_ref[...]
    xb = x.astype(BF16)
    acc = ALPHA * x
    for start, width in FF_CHUNKS:
        gate = _nn_dot(xb, wg_ref[:, start:start + width])
        up = _nn_dot(xb, wu_ref[:, start:start + width])
        h = (gate * jax.nn.sigmoid(gate) * up).astype(BF16)
        acc = acc + _nn_dot(h, wd_ref[start:start + width, :])
    y_ref[...] = _layer_norm(acc, g_ref[...], b_ref[...])


def _ffn(x2, wg, wu, wd, g, b, tm=512):
    n = x2.shape[0]
    single = pl.Buffered(1)
    return pl.pallas_call(
        _ffn_kernel,
        grid=(n // tm,),
        in_specs=[pl.BlockSpec((tm, D_MODEL), lambda i: (i, 0)),
                  pl.BlockSpec((D_MODEL, FF_DIM), lambda i: (0, 0), pipeline_mode=single),
                  pl.BlockSpec((D_MODEL, FF_DIM), lambda i: (0, 0), pipeline_mode=single),
                  pl.BlockSpec((FF_DIM, D_MODEL), lambda i: (0, 0), pipeline_mode=single),
                  pl.BlockSpec((1, D_MODEL), lambda i: (0, 0)),
                  pl.BlockSpec((1, D_MODEL), lambda i: (0, 0))],
        out_specs=pl.BlockSpec((tm, D_MODEL), lambda i: (i, 0)),
        out_shape=jax.ShapeDtypeStruct((n, D_MODEL), F32),
        compiler_params=_params(("parallel",)),
        name="ffn_ln",
    )(x2, wg, wu, wd, g, b)


C_GROUPS = {1: 1, 4: 4, 16: 4}


def kernel(x, w_in, w_o, lam, subln_g, ln1_g, ln1_b, w_gate, w_up, w_down, ln2_g, ln2_b):
    b, s, d = x.shape
    x2 = x.reshape(b * s, d)
    w_in_t = jnp.transpose(w_in, (2, 0, 1))
    for l in range(DEPTH):
        lambda_init = 0.8 - 0.6 * math.exp(-0.3 * l)
        za, zb, zc1, zc4, zc16 = _inproj(x2, _pad_w_in(w_in_t[:, l, :]), b, s)
        oa = _attn_a(za.reshape(b, s, A_COLS)).reshape(b * s, 256)
        g_pair = jnp.concatenate([subln_g[l], subln_g[l]]).reshape(1, 128).astype(F32)
        ob = _attn_b(zb.reshape(b, s, B_COLS), lam[l].astype(F32), g_pair, lambda_init).reshape(b * s, 256)
        zcs = (zc1.reshape(b, 1, s, C_COLS), zc4, zc16)
        cs, ls = zip(*[_attn_c(z, dil, C_GROUPS[dil]) for z, dil in zip(zcs, C_DILATIONS)])
        x2 = _mix(x2, oa, ob, cs, ls, w_o[l].astype(BF16),
                  ln1_g[l].reshape(1, d), ln1_b[l].reshape(1, d), s)
        x2 = _ffn(x2, w_gate[l].astype(BF16), w_up[l].astype(BF16), w_down[l].astype(BF16),
                  ln2_g[l].reshape(1, d), ln2_b[l].reshape(1, d))
    return x2.reshape(b, s, d)
```

```python
import functools
import math

import jax
import jax.numpy as jnp
from jax import lax
from jax.experimental import pallas as pl
from jax.experimental.pallas import tpu as pltpu

F32 = jnp.float32
BF16 = jnp.bfloat16
I32 = jnp.int32

D_MODEL = 1024
DEPTH = 2
HEAD_DIM = 64
A_HEADS = 4
IDX_HEADS = 4
TOPK = 256
B_HEADS = 4
DIFF_DH = 32
C_HEADS = 8
C_DILATIONS = (1, 4, 16)
C_STEPS = 128
QB = 128
LANES = 128
N_CLASSES = 4
N_CLASSES_B = 8
FF_DIM = 2816
ALPHA = (2 * DEPTH) ** 0.25
LN_EPS = 1e-5
NEG = -1e30

A_COLS = 1024
B_COLS = 768
C_COLS = 1536

VMEM_LIMIT = 56 * 1024 * 1024

INT_MIN = -2 ** 31
KEY_NEG_INF = (-8388608) ^ 0x7FFFFFFF


def _slopes(n):
    return [2.0 ** (-8.0 * (i + 1) / n) for i in range(n)]


def _nt_dot(a, b):
    return lax.dot_general(a, b, (((1,), (1,)), ((), ())), preferred_element_type=F32)


def _nn_dot(a, b):
    return jnp.dot(a, b, preferred_element_type=F32)


def _params(sem):
    return pltpu.CompilerParams(dimension_semantics=sem, vmem_limit_bytes=VMEM_LIMIT)


def _position_columns(rows, width):
    r = lax.broadcasted_iota(I32, (rows, width), 0)
    lane = lax.broadcasted_iota(I32, (rows, width), 1)
    return jnp.where(lane == 0, r & 127, jnp.where(lane == 1, r >> 7, 0)).astype(F32)


def _slope_columns(slope, rows, width):
    lane = lax.broadcasted_iota(I32, (rows, width), 1)
    return jnp.where(lane == 0, slope, jnp.where(lane == 1, slope * 128.0, 0.0)).astype(F32)


def _slab(k):
    return slice(k * LANES, (k + 1) * LANES)


def _row_sum(x):
    return jnp.sum(x, axis=1, keepdims=True)


def _softmax_map(s_ref, e_ref, linv_ref, row0, sk, mask_fn):
    rows = slice(row0, row0 + QB)

    def logits(k):
        x = s_ref[rows, _slab(k)]
        mask = mask_fn(k)
        return x if mask is None else x + mask

    nslab = sk // LANES
    mrun = logits(0)
    for k in range(1, nslab):
        mrun = jnp.maximum(mrun, logits(k))
    m = jnp.max(mrun, axis=1, keepdims=True)
    lrun = jnp.zeros((QB, LANES), F32)
    for k in range(nslab):
        e = jnp.exp(logits(k) - m)
        lrun = lrun + e
        e_ref[rows, _slab(k)] = e.astype(BF16)
    linv_ref[rows, :] = jnp.broadcast_to(1.0 / _row_sum(lrun), (QB, LANES))


def _causal_slab(k, t0):
    row_t = t0 + lax.broadcasted_iota(I32, (QB, LANES), 0)
    col = k * LANES + lax.broadcasted_iota(I32, (QB, LANES), 1)
    return col <= row_t


def _inproj_kernel(x_ref, w32_ref, za_ref, zb_ref, zc1_ref, zc4_ref, zc16_ref, acc_ref, w_ref, *, tm):
    @pl.when(pl.program_id(0) == 0)
    def _():
        for c in range(w32_ref.shape[0] // 128):
            w_ref[:, c * 128:(c + 1) * 128] = w32_ref[c * 128:(c + 1) * 128, :].T.astype(BF16)

    x = x_ref[...].astype(BF16)
    za_ref[...] = _nn_dot(x, w_ref[:, 0:A_COLS])
    zb = _nn_dot(x, w_ref[:, A_COLS:A_COLS + B_COLS])
    lane = lax.broadcasted_iota(I32, (1, B_COLS), 1)
    zb_ref[...] = (zb * jnp.where(lane < 256, DIFF_DH ** -0.5, 1.0)).astype(BF16)
    zc = _nn_dot(x, w_ref[:, A_COLS + B_COLS:])
    lane = lax.broadcasted_iota(I32, (1, C_COLS), 1)
    zc = zc * jnp.where(lane < 512, HEAD_DIM ** -0.5, 1.0)
    zc1_ref[...] = zc.astype(BF16)
    for c in range(C_COLS // 128):
        acc_ref[c] = zc[:, c * 128:(c + 1) * 128]
    for dst, r in ((zc4_ref, 4), (zc16_ref, 16)):
        for o in range(r):
            for c in range(C_COLS // 128):
                dst[0, o, :, c * 128:(c + 1) * 128] = acc_ref[c, pl.ds(o, tm // r, stride=r), :].astype(BF16)


def _inproj(x2, w_pad, b, s, tm=512):
    n = x2.shape[0]
    ncols = A_COLS + B_COLS + C_COLS
    tiles = s // tm
    row = lambda w: pl.BlockSpec((tm, w), lambda i: (i, 0))
    cls = lambda r: pl.BlockSpec((1, r, tm // r, C_COLS), lambda i: (i // tiles, 0, i % tiles, 0))
    return pl.pallas_call(
        functools.partial(_inproj_kernel, tm=tm),
        grid=(n // tm,),
        in_specs=[row(D_MODEL), pl.BlockSpec((ncols, D_MODEL), lambda i: (0, 0), pipeline_mode=pl.Buffered(1))],
        out_specs=[row(A_COLS), row(B_COLS), row(C_COLS), cls(4), cls(16)],
        out_shape=[jax.ShapeDtypeStruct((n, A_COLS), F32),
                   jax.ShapeDtypeStruct((n, B_COLS), BF16),
                   jax.ShapeDtypeStruct((n, C_COLS), BF16),
                   jax.ShapeDtypeStruct((b, 4, s // 4, C_COLS), BF16),
                   jax.ShapeDtypeStruct((b, 16, s // 16, C_COLS), BF16)],
        scratch_shapes=[pltpu.VMEM((C_COLS // 128, tm, 128), F32), pltpu.VMEM((D_MODEL, ncols), BF16)],
        compiler_params=_params(("arbitrary",)),
        name="inproj",
    )(x2, w_pad)


def _pad_w_in(wt):
    def z(k):
        return jnp.zeros((k, D_MODEL), wt.dtype)
    qa, ka, va = wt[0:256], wt[256:320], wt[320:384]
    qi, ki, wi = wt[384:640], wt[640:704], wt[704:708]
    rest = wt[708:]
    return jnp.concatenate([qa, qi, ka, z(64), va, z(64), ki, z(64), wi, z(124), rest], axis=0)


def _attn_a_class(zq_ref, o_ref, kax_ref, vab_ref, kib_ref, s_ref, key_ref, mb_ref, e_ref, linv_ref, thr_ref,
                  wb_ref, *, cls, per):
    kf = float(TOPK)
    nslabs = [cls * per + jj + 1 for jj in range(per)]

    def indexer_keys(jj):
        r0, t0, ns = jj * QB, (cls * per + jj) * QB, nslabs[jj]
        ext = ns * LANES
        zq = zq_ref[0, r0:r0 + QB, :]
        qi = zq[:, 256:512] * 0.125
        zeros64 = jnp.zeros((QB, 64), F32)
        qi_stack = jnp.concatenate(
            [jnp.concatenate([qi[:, h * 64:(h + 1) * 64], zeros64], axis=1) for h in range(IDX_HEADS)], axis=0)
        si, wb = s_ref.at[jj % 2], wb_ref.at[jj % 2]
        si[:, 0:ext] = _nt_dot(qi_stack.astype(BF16), kib_ref[0:ext, :])
        wi = zq[:, 896:900] * 0.5
        for h in range(IDX_HEADS):
            wb[h] = jnp.broadcast_to(wi[:, h:h + 1], (QB, LANES))
        for k in range(ns):
            score = jnp.maximum(si[0:QB, _slab(k)], 0.0) * wb[0]
            for h in range(1, IDX_HEADS):
                score = score + jnp.maximum(si[h * QB:(h + 1) * QB, _slab(k)], 0.0) * wb[h]
            if k == ns - 1:
                score = jnp.where(_causal_slab(k, t0), score, -jnp.inf)
            bits = lax.bitcast_convert_type(score, I32)
            key_ref[r0:r0 + QB, _slab(k)] = jnp.where(bits < 0, bits ^ 0x7FFFFFFF, bits)

    for jj in range(per):
        indexer_keys(jj)

    first_rb = 2 if cls == 0 else 0
    for rb in range(first_rb):
        thr_ref[rb * QB:(rb + 1) * QB, :] = jnp.full((QB, LANES), KEY_NEG_INF, I32)
    active = list(range(first_rb, per))

    def count_ge(rb, cand):
        rows = slice(rb * QB, (rb + 1) * QB)
        run = jnp.where(key_ref[rows, _slab(0)] >= cand, 1.0, 0.0)
        for k in range(1, nslabs[rb]):
            run = run + jnp.where(key_ref[rows, _slab(k)] >= cand, 1.0, 0.0)
        return _row_sum(run)

    for rb in active:
        zero = jnp.zeros((QB, LANES), I32)
        thr_ref[rb * QB:(rb + 1) * QB, :] = jnp.where(count_ge(rb, zero) >= kf, zero, INT_MIN)

    def search(it, carry):
        bit = lax.shift_left(jnp.int32(1), 30 - it)
        for rb in active:
            rows = slice(rb * QB, (rb + 1) * QB)
            thr = thr_ref[rows, :]
            cand = thr | bit
            thr_ref[rows, :] = jnp.where(count_ge(rb, cand) >= kf, cand, thr)
        return carry

    lax.fori_loop(0, 31, search, 0)

    def select_mask(jj, buf):
        r0, t0, ns = jj * QB, (cls * per + jj) * QB, nslabs[jj]
        rows = slice(r0, r0 + QB)
        mb = mb_ref.at[buf]
        thr = thr_ref[rows, :]
        run = jnp.zeros((QB, LANES), F32)
        for k in range(ns):
            ge = key_ref[rows, _slab(k)] >= thr
            allowed = jnp.where(_causal_slab(k, t0), 0.0, NEG) if k == ns - 1 else 0.0
            mb[:, _slab(k)] = jnp.where(ge, allowed, NEG)
            run = run + jnp.where(ge, 1.0, 0.0)
        thr1 = thr[:, 0:1]
        ambiguous = jnp.where((_row_sum(run) > kf) & (thr1 != KEY_NEG_INF), 1.0, 0.0)

        @pl.when(jnp.max(ambiguous) > 0.0)
        def _():
            r_i = lax.broadcasted_iota(I32, (LANES, LANES), 0)
            c_i = lax.broadcasted_iota(I32, (LANES, LANES), 1)
            upper = jnp.where(r_i < c_i, 1.0, 0.0).astype(BF16)
            n_gt = jnp.zeros((QB, LANES), F32)
            for k in range(ns):
                n_gt = n_gt + jnp.where(key_ref[rows, _slab(k)] > thr, 1.0, 0.0)
            need = kf - _row_sum(n_gt)
            base = jnp.zeros((QB, 1), F32)
            for k in range(ns):
                kc = key_ref[rows, _slab(k)]
                eq = kc == thr
                eqf = jnp.where(eq, 1.0, 0.0)
                rank = base + _nn_dot(eqf.astype(BF16), upper)
                sel = (kc > thr) | (eq & (rank < need))
                if k == ns - 1:
                    sel = sel & _causal_slab(k, t0)
                mb[:, _slab(k)] = jnp.where(sel, 0.0, NEG)
                base = base + _row_sum(eqf)

    def attend(jj, buf):
        r0, ext = jj * QB, nslabs[jj] * LANES
        s_buf, e_buf, linv_buf, mb = s_ref.at[buf], e_ref.at[buf], linv_ref.at[buf], mb_ref.at[buf]
        qa = zq_ref[0, r0:r0 + QB, 0:256] * 0.125
        qa_stack = jnp.concatenate(
            [jnp.concatenate([qa[:, h * 64:(h + 1) * 64], _slope_columns(slope, QB, 64)], axis=1)
             for h, slope in enumerate(_slopes(A_HEADS))], axis=0)
        s_buf[:, 0:ext] = _nt_dot(qa_stack.astype(BF16), kax_ref[0:ext, :])
        outs = []
        for h in range(A_HEADS):
            _softmax_map(s_buf, e_buf, linv_buf, h * QB, ext, lambda k: mb[:, _slab(k)])
            rows = slice(h * QB, (h + 1) * QB)
            o = _nn_dot(e_buf[rows, 0:ext], vab_ref[0:ext, :]) * linv_buf[rows, :]
            outs.append(o[:, 0:64])
        o_ref[0, r0:r0 + QB, :] = jnp.concatenate(outs, axis=1).astype(o_ref.dtype)

    for it in range(per // 2):
        for u in range(2):
            select_mask(2 * it + u, u)
        for u in range(2):
            attend(2 * it + u, u)


def _attn_a_kernel(zq_ref, ka_ref, va_ref, ki_ref, o_ref,
                   kax_ref, vab_ref, kib_ref, s_ref, key_ref, mb_ref, e_ref, linv_ref, thr_ref, wb_ref, *, seq):
    c = pl.program_id(1)

    @pl.when(c == 0)
    def _():
        kax_ref[...] = (ka_ref[0] + jnp.concatenate(
            [jnp.zeros((seq, 64), F32), _position_columns(seq, 64)], axis=1)).astype(BF16)
        vab_ref[...] = va_ref[0].astype(BF16)
        kib_ref[...] = ki_ref[0].astype(BF16)

    per = seq // QB // N_CLASSES
    for cls in range(N_CLASSES):
        @pl.when(c == cls)
        def _(cls=cls):
            _attn_a_class(zq_ref, o_ref, kax_ref, vab_ref, kib_ref, s_ref, key_ref, mb_ref, e_ref, linv_ref,
                          thr_ref, wb_ref, cls=cls, per=per)


def _attn_a(za3):
    b, s, _ = za3.shape
    rows = s // N_CLASSES
    kv_spec = lambda cb: pl.BlockSpec((1, s, 128), lambda bi, c, cb=cb: (bi, 0, cb))
    return pl.pallas_call(
        functools.partial(_attn_a_kernel, seq=s),
        grid=(b, N_CLASSES),
        in_specs=[pl.BlockSpec((1, rows, A_COLS), lambda bi, c: (bi, c, 0)),
                  kv_spec(4), kv_spec(5), kv_spec(6)],
        out_specs=pl.BlockSpec((1, rows, 256), lambda bi, c: (bi, c, 0)),
        out_shape=jax.ShapeDtypeStruct((b, s, 256), BF16),
        scratch_shapes=[pltpu.VMEM((s, 128), BF16), pltpu.VMEM((s, 128), BF16), pltpu.VMEM((s, 128), BF16),
                        pltpu.VMEM((2, 4 * QB, s), F32), pltpu.VMEM((rows, s), I32), pltpu.VMEM((2, QB, s), F32),
                        pltpu.VMEM((2, 4 * QB, s), BF16), pltpu.VMEM((2, 4 * QB, LANES), F32),
                        pltpu.VMEM((rows, LANES), I32), pltpu.VMEM((2, IDX_HEADS, QB, LANES), F32)],
        compiler_params=_params(("parallel", "arbitrary")),
        name="attn_a",
    )(za3, za3, za3, za3)


def _attn_b_tile(i, q_ref, v_ref, lam_ref, g_ref, o_ref, kx_ref, s_ref, e_ref, linv_ref, *, sk, per, lambda_init):
    nslab = sk // LANES
    first_masked = nslab - per
    t_rel = (i % per) * QB
    masks = [jnp.where(_causal_slab(k, t_rel), 0.0, NEG) for k in range(per)]
    mask_fn = lambda k: masks[k - first_masked] if k >= first_masked else None
    lam = lam_ref[...]
    lam_full = (jnp.exp(jnp.sum(lam[0:1] * lam[1:2], axis=1, keepdims=True))
                - jnp.exp(jnp.sum(lam[2:3] * lam[3:4], axis=1, keepdims=True)) + lambda_init)
    lane = lax.broadcasted_iota(I32, (QB, 128), 1)
    slopes = _slopes(B_HEADS)
    for p in range(B_HEADS // 2):
        qp = q_ref[0, :, p * 128:(p + 1) * 128]
        vp = v_ref[0, 0:sk, p * 128:(p + 1) * 128]
        q_stack = jnp.concatenate(
            [jnp.concatenate([jnp.where((lane >= g * 32) & (lane < (g + 1) * 32), qp, jnp.zeros_like(qp)),
                              _slope_columns(slopes[2 * p + g // 2], QB, 128).astype(BF16)], axis=1)
             for g in range(4)], axis=0)
        s_buf, e_buf, linv_buf = s_ref.at[p], e_ref.at[p], linv_ref.at[p]
        s_buf[:, 0:sk] = _nt_dot(q_stack, kx_ref[p, 0:sk, :])
        r = []
        for g in range(4):
            _softmax_map(s_buf, e_buf, linv_buf, g * QB, sk, mask_fn)
            rows = slice(g * QB, (g + 1) * QB)
            r.append(_nn_dot(e_buf[rows, 0:sk], vp) * linv_buf[rows, :])
        o0 = r[0] - lam_full * r[1]
        o1 = r[2] - lam_full * r[3]
        first = lane < 64
        o = jnp.where(first, o0, o1)
        sq = o * o
        ms0 = _row_sum(jnp.where(first, sq, 0.0))
        ms1 = _row_sum(jnp.where(first, 0.0, sq))
        ms = jnp.where(first, ms0, ms1) * (1.0 / HEAD_DIM)
        o = o * lax.rsqrt(ms + LN_EPS) * g_ref[...] * (1.0 - lambda_init)
        o_ref[0, :, p * 128:(p + 1) * 128] = o.astype(o_ref.dtype)


def _attn_b_kernel(q_ref, k_ref, v_ref, lam_ref, g_ref, o_ref, kx_ref, s_ref, e_ref, linv_ref, *, seq, lambda_init):
    i = pl.program_id(1)

    @pl.when(i == 0)
    def _():
        pos = _position_columns(seq, 128).astype(BF16)
        for p in range(B_HEADS // 2):
            kx_ref[p, :, 0:128] = k_ref[0, :, p * 128:(p + 1) * 128]
            kx_ref[p, :, 128:256] = pos

    per = seq // QB // N_CLASSES_B
    for c in range(N_CLASSES_B):
        @pl.when(i // per == c)
        def _(c=c):
            _attn_b_tile(i, q_ref, v_ref, lam_ref, g_ref, o_ref, kx_ref, s_ref, e_ref, linv_ref,
                         sk=(c + 1) * per * QB, per=per, lambda_init=lambda_init)


def _attn_b(zb3, lam, g_pair, lambda_init):
    b, s, _ = zb3.shape
    return pl.pallas_call(
        functools.partial(_attn_b_kernel, seq=s, lambda_init=lambda_init),
        grid=(b, s // QB),
        in_specs=[pl.BlockSpec((1, QB, 256), lambda bi, i: (bi, i, 0)),
                  pl.BlockSpec((1, s, 256), lambda bi, i: (bi, 0, 1)),
                  pl.BlockSpec((1, s, 256), lambda bi, i: (bi, 0, 2)),
                  pl.BlockSpec((4, DIFF_DH), lambda bi, i: (0, 0)),
                  pl.BlockSpec((1, 128), lambda bi, i: (0, 0))],
        out_specs=pl.BlockSpec((1, QB, 256), lambda bi, i: (bi, i, 0)),
        out_shape=jax.ShapeDtypeStruct((b, s, 256), BF16),
        scratch_shapes=[pltpu.VMEM((B_HEADS // 2, s, 256), BF16), pltpu.VMEM((B_HEADS // 2, 4 * QB, s), F32),
                        pltpu.VMEM((B_HEADS // 2, 4 * QB, s), BF16), pltpu.VMEM((B_HEADS // 2, 4 * QB, 128), F32)],
        compiler_params=_params(("parallel", "arbitrary")),
        name="attn_b",
    )(zb3, zb3, zb3, lam, g_pair)


def _c_bias_tables(dilation):
    qi = jnp.arange(128)[:, None]
    kj = jnp.arange(256)[None, :]
    slopes = jnp.asarray(_slopes(C_HEADS), F32)[:, None, None]
    d_first = qi - kj
    d_mid = qi + 128 - kj
    def tab(delta):
        valid = (delta >= 0) & (delta <= C_STEPS)
        return jnp.where(valid[None], -slopes * (delta * dilation).astype(F32)[None], NEG)
    return jnp.stack([tab(d_first), tab(d_mid)]).astype(F32)


def _attn_c_kernel(z_ref, bias_ref, o_ref, lse_ref, *, ls, groups):
    nb = ls // 128
    lane = lax.broadcasted_iota(I32, (128, 128), 1)
    first = lane < 64

    def tile(n, o, p):
        if nb == 1:
            q0, k0, kw, bsel = 0, 0, 128, 0
        else:
            q0 = pl.multiple_of(n * 128, 128)
            k0 = pl.multiple_of(jnp.maximum(n - 1, 0) * 128, 128)
            kw, bsel = 256, jnp.minimum(n, 1)
        qp = z_ref[0, o, pl.ds(q0, 128), p * 128:(p + 1) * 128]
        kp = z_ref[0, o, pl.ds(k0, kw), 512 + p * 128:512 + (p + 1) * 128]
        vp = z_ref[0, o, pl.ds(k0, kw), 1024 + p * 128:1024 + (p + 1) * 128]
        zero = jnp.zeros_like(qp)
        q_stack = jnp.concatenate([jnp.where(first, qp, zero), jnp.where(first, zero, qp)], axis=0)
        s = _nt_dot(q_stack, kp)
        es, lses, linvs = [], [], []
        for hh in range(2):
            x = s[hh * 128:(hh + 1) * 128] + bias_ref[bsel, 2 * p + hh, :, 0:kw]
            m = jnp.max(x, axis=1, keepdims=True)
            e = jnp.exp(x - m)
            l = jnp.sum(e, axis=1, keepdims=True)
            es.append(e.astype(BF16))
            lses.append(m + jnp.log(l))
            linvs.append(1.0 / l)
        r = _nn_dot(jnp.concatenate(es, axis=0), vp)
        out = jnp.where(first, r[0:128] * linvs[0], r[128:256] * linvs[1])
        lse = jnp.where(first, lses[0], lses[1])
        o_ref[0, o, pl.ds(q0, 128), p * 128:(p + 1) * 128] = out.astype(o_ref.dtype)
        lse_ref[0, o, pl.ds(q0, 128), p * 128:(p + 1) * 128] = lse

    if nb == 1:
        for o in range(groups):
            for p in range(C_HEADS // 2):
                tile(0, o, p)
    else:
        step = 4
        for o in range(groups):
            def body(j, carry, o=o):
                for dn in range(step):
                    for p in range(C_HEADS // 2):
                        tile(j * step + dn, o, p)
                return carry
            lax.fori_loop(0, nb // step, body, 0)


def _attn_c(zc, dilation, groups):
    b, _, ls, _ = zc.shape
    bias = _c_bias_tables(dilation)
    spec = lambda w: pl.BlockSpec((1, groups, ls, w), lambda bi, j: (bi, j, 0, 0))
    return pl.pallas_call(
        functools.partial(_attn_c_kernel, ls=ls, groups=groups),
        grid=(b, dilation // groups),
        in_specs=[spec(C_COLS), pl.BlockSpec((2, C_HEADS, 128, 256), lambda bi, j: (0, 0, 0, 0))],
        out_specs=[spec(512), spec(512)],
        out_shape=[jax.ShapeDtypeStruct((b, dilation, ls, 512), BF16),
                   jax.ShapeDtypeStruct((b, dilation, ls, 512), F32)],
        compiler_params=_params(("parallel", "arbitrary")),
        name=f"attn_c_d{dilation}",
    )(zc, bias)


def _layer_norm(y, g, b):
    mu = jnp.mean(y, axis=-1, keepdims=True)
    d = y - mu
    var = jnp.mean(d * d, axis=-1, keepdims=True)
    return d * lax.rsqrt(var + LN_EPS) * g + b


FF_CHUNKS = ((0, 768), (768, 768), (1536, 768), (2304, 512))


def _mix_ffn_kernel(x_ref, oa_ref, ob_ref, c1_ref, c4_ref, c16_ref, l1_ref, l4_ref, l16_ref, w_ref, g_ref, b_ref,
                    wg_ref, wu_ref, wd_ref, g2_ref, b2_ref, y_ref, c4n_ref, c16n_ref, l4n_ref, l16n_ref, *, tm):
    for src, dst, r in ((c4_ref, c4n_ref, 4), (l4_ref, l4n_ref, 4), (c16_ref, c16n_ref, 16), (l16_ref, l16n_ref, 16)):
        for o in range(r):
            for c in range(4):
                dst[c, pl.ds(o, tm // r, stride=r), :] = src[0, o, :, c * 128:(c + 1) * 128].astype(F32)
    natural = lambda ref: jnp.concatenate([ref[c] for c in range(4)], axis=1)
    l0, l1, l2 = l1_ref[...], natural(l4n_ref), natural(l16n_ref)
    m = jnp.maximum(jnp.maximum(l0, l1), l2)
    w0, w1, w2 = jnp.exp(l0 - m), jnp.exp(l1 - m), jnp.exp(l2 - m)
    oc = (w0 * c1_ref[...].astype(F32) + w1 * natural(c4n_ref) + w2 * natural(c16n_ref)) / (w0 + w1 + w2)
    mix = (_nn_dot(oa_ref[...], w_ref[0:256, :]) + _nn_dot(ob_ref[...], w_ref[256:512, :])
           + _nn_dot(oc.astype(BF16), w_ref[512:1024, :]))
    x1 = _layer_norm(ALPHA * x_ref[...] + mix, g_ref[...], b_ref[...])
    xb = x1.astype(BF16)
    acc = ALPHA * x1
    for start, width in FF_CHUNKS:
        gate = _nn_dot(xb, wg_ref[:, start:start + width])
        up = _nn_dot(xb, wu_ref[:, start:start + width])
        h = (gate * jax.nn.sigmoid(gate) * up).astype(BF16)
        acc = acc + _nn_dot(h, wd_ref[start:start + width, :])
    y_ref[...] = _layer_norm(acc, g2_ref[...], b2_ref[...])


def _mix_ffn(x2, oa, ob, cs, ls, w_o, g1, b1, wg, wu, wd, g2, b2, s, tm=512):
    n = x2.shape[0]
    tiles = s // tm
    row = lambda w: pl.BlockSpec((tm, w), lambda i: (i, 0))
    cls = lambda r: pl.BlockSpec((1, r, tm // r, 512), lambda i: (i // tiles, 0, i % tiles, 0))
    const = lambda r, c: pl.BlockSpec((r, c), lambda i: (0, 0), pipeline_mode=pl.Buffered(1))
    vec = const(1, D_MODEL)
    return pl.pallas_call(
        functools.partial(_mix_ffn_kernel, tm=tm),
        grid=(n // tm,),
        in_specs=[row(D_MODEL), row(256), row(256), row(512), cls(4), cls(16), row(512), cls(4), cls(16),
                  const(D_MODEL, D_MODEL), vec, vec,
                  const(D_MODEL, FF_DIM), const(D_MODEL, FF_DIM), const(FF_DIM, D_MODEL), vec, vec],
        out_specs=row(D_MODEL),
        out_shape=jax.ShapeDtypeStruct((n, D_MODEL), F32),
        scratch_shapes=[pltpu.VMEM((4, tm, 128), F32)] * 4,
        compiler_params=_params(("parallel",)),
        name="mix_ffn_ln",
    )(x2, oa, ob, cs[0].reshape(n, 512), cs[1], cs[2], ls[0].reshape(n, 512), ls[1], ls[2], w_o, g1, b1,
      wg, wu, wd, g2, b2)


C_GROUPS = {1: 1, 4: 4, 16: 4}


def kernel(x, w_in, w_o, lam, subln_g, ln1_g, ln1_b, w_gate, w_up, w_down, ln2_g, ln2_b):
    b, s, d = x.shape
    x2 = x.reshape(b * s, d)
    w_in_t = jnp.transpose(w_in, (2, 0, 1))
    for l in range(DEPTH):
        lambda_init = 0.8 - 0.6 * math.exp(-0.3 * l)
        za, zb, zc1, zc4, zc16 = _inproj(x2, _pad_w_in(w_in_t[:, l, :]), b, s)
        oa = _attn_a(za.reshape(b, s, A_COLS)).reshape(b * s, 256)
        g_pair = jnp.concatenate([subln_g[l], subln_g[l]]).reshape(1, 128).astype(F32)
        ob = _attn_b(zb.reshape(b, s, B_COLS), lam[l].astype(F32), g_pair, lambda_init).reshape(b * s, 256)
        zcs = (zc1.reshape(b, 1, s, C_COLS), zc4, zc16)
        cs, ls = zip(*[_attn_c(z, dil, C_GROUPS[dil]) for z, dil in zip(zcs, C_DILATIONS)])
        x2 = _mix_ffn(x2, oa, ob, cs, ls, w_o[l].astype(BF16), ln1_g[l].reshape(1, d), ln1_b[l].reshape(1, d),
                      w_gate[l].astype(BF16), w_up[l].astype(BF16), w_down[l].astype(BF16),
                      ln2_g[l].reshape(1, d), ln2_b[l].reshape(1, d), s)
    return x2.reshape(b, s, d)
```

```python
import functools
import math

import jax
import jax.numpy as jnp
from jax import lax
from jax.experimental import pallas as pl
from jax.experimental.pallas import tpu as pltpu

F32 = jnp.float32
BF16 = jnp.bfloat16
I32 = jnp.int32

D_MODEL = 1024
DEPTH = 2
HEAD_DIM = 64
A_HEADS = 4
IDX_HEADS = 4
TOPK = 256
B_HEADS = 4
DIFF_DH = 32
C_HEADS = 8
C_DILATIONS = (1, 4, 16)
C_STEPS = 128
QB = 128
LANES = 128
N_CLASSES = 4
N_CLASSES_B = 8
FF_DIM = 2816
ALPHA = (2 * DEPTH) ** 0.25
LN_EPS = 1e-5
NEG = -1e30

A_COLS = 1024
B_COLS = 768
C_COLS = 1536

VMEM_LIMIT = 56 * 1024 * 1024

INT_MIN = -2 ** 31
KEY_NEG_INF = (-8388608) ^ 0x7FFFFFFF


def _slopes(n):
    return [2.0 ** (-8.0 * (i + 1) / n) for i in range(n)]


def _nt_dot(a, b):
    return lax.dot_general(a, b, (((1,), (1,)), ((), ())), preferred_element_type=F32)


def _nn_dot(a, b):
    return jnp.dot(a, b, preferred_element_type=F32)


def _params(sem):
    return pltpu.CompilerParams(dimension_semantics=sem, vmem_limit_bytes=VMEM_LIMIT)


def _position_columns(rows, width):
    r = lax.broadcasted_iota(I32, (rows, width), 0)
    lane = lax.broadcasted_iota(I32, (rows, width), 1)
    return jnp.where(lane == 0, r & 127, jnp.where(lane == 1, r >> 7, 0)).astype(F32)


def _slope_columns(slope, rows, width):
    lane = lax.broadcasted_iota(I32, (rows, width), 1)
    return jnp.where(lane == 0, slope, jnp.where(lane == 1, slope * 128.0, 0.0)).astype(F32)


def _slab(k):
    return slice(k * LANES, (k + 1) * LANES)


def _row_sum(x):
    return jnp.sum(x, axis=1, keepdims=True)


def _softmax_map(s_ref, e_ref, linv_ref, row0, sk, mask_fn):
    rows = slice(row0, row0 + QB)

    def logits(k):
        x = s_ref[rows, _slab(k)]
        mask = mask_fn(k)
        return x if mask is None else x + mask

    nslab = sk // LANES
    mrun = logits(0)
    for k in range(1, nslab):
        mrun = jnp.maximum(mrun, logits(k))
    m = jnp.max(mrun, axis=1, keepdims=True)
    lrun = jnp.zeros((QB, LANES), F32)
    for k in range(nslab):
        e = jnp.exp(logits(k) - m)
        lrun = lrun + e
        e_ref[rows, _slab(k)] = e.astype(BF16)
    linv_ref[rows, :] = jnp.broadcast_to(1.0 / _row_sum(lrun), (QB, LANES))


def _causal_slab(k, t0):
    row_t = t0 + lax.broadcasted_iota(I32, (QB, LANES), 0)
    col = k * LANES + lax.broadcasted_iota(I32, (QB, LANES), 1)
    return col <= row_t


def _inproj_kernel(x_ref, w32_ref, za_ref, zb_ref, zc1_ref, zc4_ref, zc16_ref, acc_ref, w_ref, *, tm):
    @pl.when(pl.program_id(0) == 0)
    def _():
        for c in range(w32_ref.shape[0] // 128):
            w_ref[:, c * 128:(c + 1) * 128] = w32_ref[c * 128:(c + 1) * 128, :].T.astype(BF16)

    x = x_ref[...].astype(BF16)
    za_ref[...] = _nn_dot(x, w_ref[:, 0:A_COLS])
    zb = _nn_dot(x, w_ref[:, A_COLS:A_COLS + B_COLS])
    lane = lax.broadcasted_iota(I32, (1, B_COLS), 1)
    zb_ref[...] = (zb * jnp.where(lane < 256, DIFF_DH ** -0.5, 1.0)).astype(BF16)
    zc = _nn_dot(x, w_ref[:, A_COLS + B_COLS:])
    lane = lax.broadcasted_iota(I32, (1, C_COLS), 1)
    zc = zc * jnp.where(lane < 512, HEAD_DIM ** -0.5, 1.0)
    zc1_ref[...] = zc.astype(BF16)
    for c in range(C_COLS // 128):
        acc_ref[c] = zc[:, c * 128:(c + 1) * 128]
    for dst, r in ((zc4_ref, 4), (zc16_ref, 16)):
        for o in range(r):
            for c in range(C_COLS // 128):
                dst[0, o, :, c * 128:(c + 1) * 128] = acc_ref[c, pl.ds(o, tm // r, stride=r), :].astype(BF16)


def _inproj(x2, w_pad, b, s, tm=512):
    n = x2.shape[0]
    ncols = A_COLS + B_COLS + C_COLS
    tiles = s // tm
    row = lambda w: pl.BlockSpec((tm, w), lambda i: (i, 0))
    cls = lambda r: pl.BlockSpec((1, r, tm // r, C_COLS), lambda i: (i // tiles, 0, i % tiles, 0))
    return pl.pallas_call(
        functools.partial(_inproj_kernel, tm=tm),
        grid=(n // tm,),
        in_specs=[row(D_MODEL), pl.BlockSpec((ncols, D_MODEL), lambda i: (0, 0), pipeline_mode=pl.Buffered(1))],
        out_specs=[row(A_COLS), row(B_COLS), row(C_COLS), cls(4), cls(16)],
        out_shape=[jax.ShapeDtypeStruct((n, A_COLS), F32),
                   jax.ShapeDtypeStruct((n, B_COLS), BF16),
                   jax.ShapeDtypeStruct((n, C_COLS), BF16),
                   jax.ShapeDtypeStruct((b, 4, s // 4, C_COLS), BF16),
                   jax.ShapeDtypeStruct((b, 16, s // 16, C_COLS), BF16)],
        scratch_shapes=[pltpu.VMEM((C_COLS // 128, tm, 128), F32), pltpu.VMEM((D_MODEL, ncols), BF16)],
        compiler_params=_params(("arbitrary",)),
        name="inproj",
    )(x2, w_pad)


def _pad_w_in(wt):
    def z(k):
        return jnp.zeros((k, D_MODEL), wt.dtype)
    qa, ka, va = wt[0:256], wt[256:320], wt[320:384]
    qi, ki, wi = wt[384:640], wt[640:704], wt[704:708]
    rest = wt[708:]
    return jnp.concatenate([qa, qi, ka, z(64), va, z(64), ki, z(64), wi, z(124), rest], axis=0)


def _attn_a_class(zq_ref, o_ref, kax_ref, vab_ref, kib_ref, s_ref, key_ref, mb_ref, e_ref, linv_ref, thr_ref,
                  wb_ref, *, cls, per):
    kf = float(TOPK)
    nslabs = [cls * per + jj + 1 for jj in range(per)]

    def indexer_keys(jj):
        r0, t0, ns = jj * QB, (cls * per + jj) * QB, nslabs[jj]
        ext = ns * LANES
        zq = zq_ref[0, r0:r0 + QB, :]
        qi = zq[:, 256:512] * 0.125
        zeros64 = jnp.zeros((QB, 64), F32)
        qi_stack = jnp.concatenate(
            [jnp.concatenate([qi[:, h * 64:(h + 1) * 64], zeros64], axis=1) for h in range(IDX_HEADS)], axis=0)
        si, wb = s_ref.at[jj % 2], wb_ref.at[jj % 2]
        si[:, 0:ext] = _nt_dot(qi_stack.astype(BF16), kib_ref[0:ext, :])
        wi = zq[:, 896:900] * 0.5
        for h in range(IDX_HEADS):
            wb[h] = jnp.broadcast_to(wi[:, h:h + 1], (QB, LANES))
        for k in range(ns):
            score = jnp.maximum(si[0:QB, _slab(k)], 0.0) * wb[0]
            for h in range(1, IDX_HEADS):
                score = score + jnp.maximum(si[h * QB:(h + 1) * QB, _slab(k)], 0.0) * wb[h]
            if k == ns - 1:
                score = jnp.where(_causal_slab(k, t0), score, -jnp.inf)
            bits = lax.bitcast_convert_type(score, I32)
            key_ref[r0:r0 + QB, _slab(k)] = jnp.where(bits < 0, bits ^ 0x7FFFFFFF, bits)

    for jj in range(per):
        indexer_keys(jj)

    first_rb = 2 if cls == 0 else 0
    for rb in range(first_rb):
        thr_ref[rb * QB:(rb + 1) * QB, :] = jnp.full((QB, LANES), KEY_NEG_INF, I32)
    active = list(range(first_rb, per))

    def count_ge(rb, cand):
        rows = slice(rb * QB, (rb + 1) * QB)
        run = jnp.where(key_ref[rows, _slab(0)] >= cand, 1.0, 0.0)
        for k in range(1, nslabs[rb]):
            run = run + jnp.where(key_ref[rows, _slab(k)] >= cand, 1.0, 0.0)
        return _row_sum(run)

    for rb in active:
        zero = jnp.zeros((QB, LANES), I32)
        thr_ref[rb * QB:(rb + 1) * QB, :] = jnp.where(count_ge(rb, zero) >= kf, zero, INT_MIN)

    def search(it, carry):
        bit = lax.shift_left(jnp.int32(1), 30 - it)
        for rb in active:
            rows = slice(rb * QB, (rb + 1) * QB)
            thr = thr_ref[rows, :]
            cand = thr | bit
            thr_ref[rows, :] = jnp.where(count_ge(rb, cand) >= kf, cand, thr)
        return carry

    lax.fori_loop(0, 31, search, 0)

    def select_mask(jj, buf):
        r0, t0, ns = jj * QB, (cls * per + jj) * QB, nslabs[jj]
        rows = slice(r0, r0 + QB)
        mb = mb_ref.at[buf]
        thr = thr_ref[rows, :]
        run = jnp.zeros((QB, LANES), F32)
        for k in range(ns):
            ge = key_ref[rows, _slab(k)] >= thr
            allowed = jnp.where(_causal_slab(k, t0), 0.0, NEG) if k == ns - 1 else 0.0
            mb[:, _slab(k)] = jnp.where(ge, allowed, NEG)
            run = run + jnp.where(ge, 1.0, 0.0)
        thr1 = thr[:, 0:1]
        ambiguous = jnp.where((_row_sum(run) > kf) & (thr1 != KEY_NEG_INF), 1.0, 0.0)

        @pl.when(jnp.max(ambiguous) > 0.0)
        def _():
            r_i = lax.broadcasted_iota(I32, (LANES, LANES), 0)
            c_i = lax.broadcasted_iota(I32, (LANES, LANES), 1)
            upper = jnp.where(r_i < c_i, 1.0, 0.0).astype(BF16)
            n_gt = jnp.zeros((QB, LANES), F32)
            for k in range(ns):
                n_gt = n_gt + jnp.where(key_ref[rows, _slab(k)] > thr, 1.0, 0.0)
            need = kf - _row_sum(n_gt)
            base = jnp.zeros((QB, 1), F32)
            for k in range(ns):
                kc = key_ref[rows, _slab(k)]
                eq = kc == thr
                eqf = jnp.where(eq, 1.0, 0.0)
                rank = base + _nn_dot(eqf.astype(BF16), upper)
                sel = (kc > thr) | (eq & (rank < need))
                if k == ns - 1:
                    sel = sel & _causal_slab(k, t0)
                mb[:, _slab(k)] = jnp.where(sel, 0.0, NEG)
                base = base + _row_sum(eqf)

    def attend(jj, buf):
        r0, ext = jj * QB, nslabs[jj] * LANES
        s_buf, e_buf, linv_buf, mb = s_ref.at[buf % 2], e_ref.at[buf % 2], linv_ref.at[buf % 2], mb_ref.at[buf]
        qa = zq_ref[0, r0:r0 + QB, 0:256] * 0.125
        qa_stack = jnp.concatenate(
            [jnp.concatenate([qa[:, h * 64:(h + 1) * 64], _slope_columns(slope, QB, 64)], axis=1)
             for h, slope in enumerate(_slopes(A_HEADS))], axis=0)
        s_buf[:, 0:ext] = _nt_dot(qa_stack.astype(BF16), kax_ref[0:ext, :])
        outs = []
        for h in range(A_HEADS):
            _softmax_map(s_buf, e_buf, linv_buf, h * QB, ext, lambda k: mb[:, _slab(k)])
            rows = slice(h * QB, (h + 1) * QB)
            o = _nn_dot(e_buf[rows, 0:ext], vab_ref[0:ext, :]) * linv_buf[rows, :]
            outs.append(o[:, 0:64])
        o_ref[0, r0:r0 + QB, :] = jnp.concatenate(outs, axis=1).astype(o_ref.dtype)

    for jj in range(per):
        select_mask(jj, jj)
    for jj in range(per):
        attend(jj, jj)


def _attn_a_kernel(zq_ref, ka_ref, va_ref, ki_ref, o_ref,
                   kax_ref, vab_ref, kib_ref, s_ref, key_ref, mb_ref, e_ref, linv_ref, thr_ref, wb_ref, *, seq):
    c = pl.program_id(1)

    @pl.when(c == 0)
    def _():
        kax_ref[...] = (ka_ref[0] + jnp.concatenate(
            [jnp.zeros((seq, 64), F32), _position_columns(seq, 64)], axis=1)).astype(BF16)
        vab_ref[...] = va_ref[0].astype(BF16)
        kib_ref[...] = ki_ref[0].astype(BF16)

    per = seq // QB // N_CLASSES
    for cls in range(N_CLASSES):
        @pl.when(c == cls)
        def _(cls=cls):
            _attn_a_class(zq_ref, o_ref, kax_ref, vab_ref, kib_ref, s_ref, key_ref, mb_ref, e_ref, linv_ref,
                          thr_ref, wb_ref, cls=cls, per=per)


def _attn_a(za3):
    b, s, _ = za3.shape
    rows = s // N_CLASSES
    kv_spec = lambda cb: pl.BlockSpec((1, s, 128), lambda bi, c, cb=cb: (bi, 0, cb))
    return pl.pallas_call(
        functools.partial(_attn_a_kernel, seq=s),
        grid=(b, N_CLASSES),
        in_specs=[pl.BlockSpec((1, rows, A_COLS), lambda bi, c: (bi, c, 0)),
                  kv_spec(4), kv_spec(5), kv_spec(6)],
        out_specs=pl.BlockSpec((1, rows, 256), lambda bi, c: (bi, c, 0)),
        out_shape=jax.ShapeDtypeStruct((b, s, 256), BF16),
        scratch_shapes=[pltpu.VMEM((s, 128), BF16), pltpu.VMEM((s, 128), BF16), pltpu.VMEM((s, 128), BF16),
                        pltpu.VMEM((2, 4 * QB, s), F32), pltpu.VMEM((rows, s), I32), pltpu.VMEM((rows // QB, QB, s), F32),
                        pltpu.VMEM((2, 4 * QB, s), BF16), pltpu.VMEM((2, 4 * QB, LANES), F32),
                        pltpu.VMEM((rows, LANES), I32), pltpu.VMEM((2, IDX_HEADS, QB, LANES), F32)],
        compiler_params=_params(("parallel", "arbitrary")),
        name="attn_a",
    )(za3, za3, za3, za3)


def _attn_b_tile(i, q_ref, v_ref, lam_ref, g_ref, o_ref, kx_ref, s_ref, e_ref, linv_ref, *, sk, per, lambda_init):
    nslab = sk // LANES
    first_masked = nslab - per
    t_rel = (i % per) * QB
    masks = [jnp.where(_causal_slab(k, t_rel), 0.0, NEG) for k in range(per)]
    mask_fn = lambda k: masks[k - first_masked] if k >= first_masked else None
    lam = lam_ref[...]
    lam_full = (jnp.exp(jnp.sum(lam[0:1] * lam[1:2], axis=1, keepdims=True))
                - jnp.exp(jnp.sum(lam[2:3] * lam[3:4], axis=1, keepdims=True)) + lambda_init)
    lane = lax.broadcasted_iota(I32, (QB, 128), 1)
    slopes = _slopes(B_HEADS)
    for p in range(B_HEADS // 2):
        qp = q_ref[0, :, p * 128:(p + 1) * 128]
        vp = v_ref[0, 0:sk, p * 128:(p + 1) * 128]
        q_stack = jnp.concatenate(
            [jnp.concatenate([jnp.where((lane >= g * 32) & (lane < (g + 1) * 32), qp, jnp.zeros_like(qp)),
                              _slope_columns(slopes[2 * p + g // 2], QB, 128).astype(BF16)], axis=1)
             for g in range(4)], axis=0)
        s_buf, e_buf, linv_buf = s_ref.at[p], e_ref.at[p], linv_ref.at[p]
        s_buf[:, 0:sk] = _nt_dot(q_stack, kx_ref[p, 0:sk, :])
        r = []
        for g in range(4):
            _softmax_map(s_buf, e_buf, linv_buf, g * QB, sk, mask_fn)
            if g % 2 == 1:
                rows = slice((g - 1) * QB, (g + 1) * QB)
                rr = _nn_dot(e_buf[rows, 0:sk], vp) * linv_buf[rows, :]
                r += [rr[0:QB], rr[QB:2 * QB]]
        o0 = r[0] - lam_full * r[1]
        o1 = r[2] - lam_full * r[3]
        first = lane < 64
        o = jnp.where(first, o0, o1)
        sq = o * o
        ms0 = _row_sum(jnp.where(first, sq, 0.0))
        ms1 = _row_sum(jnp.where(first, 0.0, sq))
        ms = jnp.where(first, ms0, ms1) * (1.0 / HEAD_DIM)
        o = o * lax.rsqrt(ms + LN_EPS) * g_ref[...] * (1.0 - lambda_init)
        o_ref[0, :, p * 128:(p + 1) * 128] = o.astype(o_ref.dtype)


def _attn_b_kernel(q_ref, k_ref, v_ref, lam_ref, g_ref, o_ref, kx_ref, s_ref, e_ref, linv_ref, *, seq, lambda_init):
    i = pl.program_id(1)

    @pl.when(i == 0)
    def _():
        pos = _position_columns(seq, 128).astype(BF16)
        for p in range(B_HEADS // 2):
            kx_ref[p, :, 0:128] = k_ref[0, :, p * 128:(p + 1) * 128]
            kx_ref[p, :, 128:256] = pos

    per = seq // QB // N_CLASSES_B
    for c in range(N_CLASSES_B):
        @pl.when(i // per == c)
        def _(c=c):
            _attn_b_tile(i, q_ref, v_ref, lam_ref, g_ref, o_ref, kx_ref, s_ref, e_ref, linv_ref,
                         sk=(c + 1) * per * QB, per=per, lambda_init=lambda_init)


def _attn_b(zb3, lam, g_pair, lambda_init):
    b, s, _ = zb3.shape
    return pl.pallas_call(
        functools.partial(_attn_b_kernel, seq=s, lambda_init=lambda_init),
        grid=(b, s // QB),
        in_specs=[pl.BlockSpec((1, QB, 256), lambda bi, i: (bi, i, 0)),
                  pl.BlockSpec((1, s, 256), lambda bi, i: (bi, 0, 1)),
                  pl.BlockSpec((1, s, 256), lambda bi, i: (bi, 0, 2)),
                  pl.BlockSpec((4, DIFF_DH), lambda bi, i: (0, 0)),
                  pl.BlockSpec((1, 128), lambda bi, i: (0, 0))],
        out_specs=pl.BlockSpec((1, QB, 256), lambda bi, i: (bi, i, 0)),
        out_shape=jax.ShapeDtypeStruct((b, s, 256), BF16),
        scratch_shapes=[pltpu.VMEM((B_HEADS // 2, s, 256), BF16), pltpu.VMEM((B_HEADS // 2, 4 * QB, s), F32),
                        pltpu.VMEM((B_HEADS // 2, 4 * QB, s), BF16), pltpu.VMEM((B_HEADS // 2, 4 * QB, 128), F32)],
        compiler_params=_params(("parallel", "arbitrary")),
        name="attn_b",
    )(zb3, zb3, zb3, lam, g_pair)


def _c_bias_tables(dilation):
    qi = jnp.arange(128)[:, None]
    kj = jnp.arange(256)[None, :]
    slopes = jnp.asarray(_slopes(C_HEADS), F32)[:, None, None]
    d_first = qi - kj
    d_mid = qi + 128 - kj
    def tab(delta):
        valid = (delta >= 0) & (delta <= C_STEPS)
        return jnp.where(valid[None], -slopes * (delta * dilation).astype(F32)[None], NEG)
    return jnp.stack([tab(d_first), tab(d_mid)]).astype(F32)


def _attn_c_kernel(z_ref, bias_ref, o_ref, lse_ref, *, ls, groups):
    nb = ls // 128
    lane = lax.broadcasted_iota(I32, (128, 128), 1)
    first = lane < 64

    def tile(n, o, p):
        if nb == 1:
            q0, k0, kw, bsel = 0, 0, 128, 0
        else:
            q0 = pl.multiple_of(n * 128, 128)
            k0 = pl.multiple_of(jnp.maximum(n - 1, 0) * 128, 128)
            kw, bsel = 256, jnp.minimum(n, 1)
        qp = z_ref[0, o, pl.ds(q0, 128), p * 128:(p + 1) * 128]
        kp = z_ref[0, o, pl.ds(k0, kw), 512 + p * 128:512 + (p + 1) * 128]
        vp = z_ref[0, o, pl.ds(k0, kw), 1024 + p * 128:1024 + (p + 1) * 128]
        zero = jnp.zeros_like(qp)
        q_stack = jnp.concatenate([jnp.where(first, qp, zero), jnp.where(first, zero, qp)], axis=0)
        s = _nt_dot(q_stack, kp)
        es, lses, linvs = [], [], []
        for hh in range(2):
            x = s[hh * 128:(hh + 1) * 128] + bias_ref[bsel, 2 * p + hh, :, 0:kw]
            m = jnp.max(x, axis=1, keepdims=True)
            e = jnp.exp(x - m)
            l = jnp.sum(e, axis=1, keepdims=True)
            es.append(e.astype(BF16))
            lses.append(m + jnp.log(l))
            linvs.append(1.0 / l)
        r = _nn_dot(jnp.concatenate(es, axis=0), vp)
        out = jnp.where(first, r[0:128] * linvs[0], r[128:256] * linvs[1])
        lse = jnp.where(first, lses[0], lses[1])
        o_ref[0, o, pl.ds(q0, 128), p * 128:(p + 1) * 128] = out.astype(o_ref.dtype)
        lse_ref[0, o, pl.ds(q0, 128), p * 128:(p + 1) * 128] = lse

    if nb == 1:
        for o in range(groups):
            for p in range(C_HEADS // 2):
                tile(0, o, p)
    else:
        step = 4
        for o in range(groups):
            def body(j, carry, o=o):
                for dn in range(step):
                    for p in range(C_HEADS // 2):
                        tile(j * step + dn, o, p)
                return carry
            lax.fori_loop(0, nb // step, body, 0)


def _attn_c(zc, dilation, groups):
    b, _, ls, _ = zc.shape
    bias = _c_bias_tables(dilation)
    spec = lambda w: pl.BlockSpec((1, groups, ls, w), lambda bi, j: (bi, j, 0, 0))
    return pl.pallas_call(
        functools.partial(_attn_c_kernel, ls=ls, groups=groups),
        grid=(b, dilation // groups),
        in_specs=[spec(C_COLS), pl.BlockSpec((2, C_HEADS, 128, 256), lambda bi, j: (0, 0, 0, 0))],
        out_specs=[spec(512), spec(512)],
        out_shape=[jax.ShapeDtypeStruct((b, dilation, ls, 512), BF16),
                   jax.ShapeDtypeStruct((b, dilation, ls, 512), F32)],
        compiler_params=_params(("parallel", "arbitrary")),
        name=f"attn_c_d{dilation}",
    )(zc, bias)


def _layer_norm(y, g, b):
    mu = jnp.mean(y, axis=-1, keepdims=True)
    d = y - mu
    var = jnp.mean(d * d, axis=-1, keepdims=True)
    return d * lax.rsqrt(var + LN_EPS) * g + b


FF_CHUNKS = ((0, 768), (768, 768), (1536, 768), (2304, 512))


def _mix_ffn_kernel(x_ref, oa_ref, ob_ref, c1_ref, c4_ref, c16_ref, l1_ref, l4_ref, l16_ref, w_ref, g_ref, b_ref,
                    wg_ref, wu_ref, wd_ref, g2_ref, b2_ref, y_ref, c4n_ref, c16n_ref, l4n_ref, l16n_ref, *, tm):
    for src, dst, r in ((c4_ref, c4n_ref, 4), (l4_ref, l4n_ref, 4), (c16_ref, c16n_ref, 16), (l16_ref, l16n_ref, 16)):
        for o in range(r):
            for c in range(4):
                dst[c, pl.ds(o, tm // r, stride=r), :] = src[0, o, :, c * 128:(c + 1) * 128].astype(F32)
    natural = lambda ref: jnp.concatenate([ref[c] for c in range(4)], axis=1)
    l0, l1, l2 = l1_ref[...], natural(l4n_ref), natural(l16n_ref)
    m = jnp.maximum(jnp.maximum(l0, l1), l2)
    w0, w1, w2 = jnp.exp(l0 - m), jnp.exp(l1 - m), jnp.exp(l2 - m)
    oc = (w0 * c1_ref[...].astype(F32) + w1 * natural(c4n_ref) + w2 * natural(c16n_ref)) / (w0 + w1 + w2)
    mix = (_nn_dot(oa_ref[...], w_ref[0:256, :]) + _nn_dot(ob_ref[...], w_ref[256:512, :])
           + _nn_dot(oc.astype(BF16), w_ref[512:1024, :]))
    x1 = _layer_norm(ALPHA * x_ref[...] + mix, g_ref[...], b_ref[...])
    xb = x1.astype(BF16)
    acc = ALPHA * x1
    for start, width in FF_CHUNKS:
        gate = _nn_dot(xb, wg_ref[:, start:start + width])
        up = _nn_dot(xb, wu_ref[:, start:start + width])
        h = (gate * jax.nn.sigmoid(gate) * up).astype(BF16)
        acc = acc + _nn_dot(h, wd_ref[start:start + width, :])
    y_ref[...] = _layer_norm(acc, g2_ref[...], b2_ref[...])


def _mix_ffn(x2, oa, ob, cs, ls, w_o, g1, b1, wg, wu, wd, g2, b2, s, tm=512):
    n = x2.shape[0]
    tiles = s // tm
    row = lambda w: pl.BlockSpec((tm, w), lambda i: (i, 0))
    cls = lambda r: pl.BlockSpec((1, r, tm // r, 512), lambda i: (i // tiles, 0, i % tiles, 0))
    const = lambda r, c: pl.BlockSpec((r, c), lambda i: (0, 0), pipeline_mode=pl.Buffered(1))
    vec = const(1, D_MODEL)
    return pl.pallas_call(
        functools.partial(_mix_ffn_kernel, tm=tm),
        grid=(n // tm,),
        in_specs=[row(D_MODEL), row(256), row(256), row(512), cls(4), cls(16), row(512), cls(4), cls(16),
                  const(D_MODEL, D_MODEL), vec, vec,
                  const(D_MODEL, FF_DIM), const(D_MODEL, FF_DIM), const(FF_DIM, D_MODEL), vec, vec],
        out_specs=row(D_MODEL),
        out_shape=jax.ShapeDtypeStruct((n, D_MODEL), F32),
        scratch_shapes=[pltpu.VMEM((4, tm, 128), F32)] * 4,
        compiler_params=_params(("parallel",)),
        name="mix_ffn_ln",
    )(x2, oa, ob, cs[0].reshape(n, 512), cs[1], cs[2], ls[0].reshape(n, 512), ls[1], ls[2], w_o, g1, b1,
      wg, wu, wd, g2, b2)


C_GROUPS = {1: 1, 4: 4, 16: 4}


def kernel(x, w_in, w_o, lam, subln_g, ln1_g, ln1_b, w_gate, w_up, w_down, ln2_g, ln2_b):
    b, s, d = x.shape
    x2 = x.reshape(b * s, d)
    w_in_t = jnp.transpose(w_in, (2, 0, 1))
    for l in range(DEPTH):
        lambda_init = 0.8 - 0.6 * math.exp(-0.3 * l)
        za, zb, zc1, zc4, zc16 = _inproj(x2, _pad_w_in(w_in_t[:, l, :]), b, s)
        oa = _attn_a(za.reshape(b, s, A_COLS)).reshape(b * s, 256)
        g_pair = jnp.concatenate([subln_g[l], subln_g[l]]).reshape(1, 128).astype(F32)
        ob = _attn_b(zb.reshape(b, s, B_COLS), lam[l].astype(F32), g_pair, lambda_init).reshape(b * s, 256)
        zcs = (zc1.reshape(b, 1, s, C_COLS), zc4, zc16)
        cs, ls = zip(*[_attn_c(z, dil, C_GROUPS[dil]) for z, dil in zip(zcs, C_DILATIONS)])
        x2 = _mix_ffn(x2, oa, ob, cs, ls, w_o[l].astype(BF16), ln1_g[l].reshape(1, d), ln1_b[l].reshape(1, d),
                      w_gate[l].astype(BF16), w_up[l].astype(BF16), w_down[l].astype(BF16),
                      ln2_g[l].reshape(1, d), ln2_b[l].reshape(1, d), s)
    return x2.reshape(b, s, d)
```
